```python
import math
import jax
import jax.numpy as jnp
from jax import lax
import numpy as np

D_MODEL = 2048
BATCH = 16
SEQ = 256
DEPTH = 4
DEC_BATCH = 4
DEC_SEQ = 2048
PAST_LEN = 256

GRID_W = 64
Q_BLOCK = 128
WINDOW = 128
A_HEADS = 8
A_KV_HEADS = 2
A_GROUP = A_HEADS // A_KV_HEADS
HEAD_DIM = 128
B_HEADS = 4
B_HEAD_DIM = 128
B_V_DIM = 2 * B_HEAD_DIM
C_HEADS = 8
C_NOPE = 128
C_ROPE = 64
C_VDIM = 128
C_Q_LORA = 512
C_KV_LORA = 256
N_BRANCH = 3
N_EXPERTS = 32
TOP_K = 4
D_FF = 2048
SWIGLU_LIMIT = 7.0
SWIGLU_ALPHA = 1.702
MOE_BLOCK = 128
ROPE_BASE = 10000.0
LN_EPS = 1e-5
RMS_EPS = 1e-6
NEG_INF = -1e30
DN_ALPHA = (2 * DEPTH) ** 0.25
DN_BETA = (8 * DEPTH) ** -0.25
IN_SIZES = (A_HEADS * HEAD_DIM, A_KV_HEADS * HEAD_DIM, A_KV_HEADS * HEAD_DIM,
            2 * B_HEADS * B_HEAD_DIM, 2 * B_HEADS * B_HEAD_DIM, B_HEADS * B_V_DIM,
            C_Q_LORA, C_KV_LORA, C_ROPE, N_BRANCH * D_MODEL)
D_IN = sum(IN_SIZES)

kernel_name = "hybrid_prefix_diffusion_step"


def layer_norm(x, g=None, b=None):
    xf = x.astype(jnp.float32)
    mu = jnp.mean(xf, -1, keepdims=True)
    var = jnp.mean(jnp.square(xf - mu), -1, keepdims=True)
    y = (xf - mu) * lax.rsqrt(var + LN_EPS)
    if g is not None:
        y = y * g.astype(jnp.float32) + b.astype(jnp.float32)
    return y.astype(x.dtype)


def rms_norm(x, g):
    xf = x.astype(jnp.float32)
    y = xf * lax.rsqrt(jnp.mean(jnp.square(xf), -1, keepdims=True) + RMS_EPS)
    return (y * g.astype(jnp.float32)).astype(x.dtype)


def adaln(cond, w_ada, b_ada):
    m = jax.nn.silu(cond) @ w_ada + b_ada
    return m.reshape(cond.shape[0], 6, D_MODEL)


def modulate(x, m, i):
    return layer_norm(x) * (1.0 + m[:, i + 1][:, None, :]) + m[:, i][:, None, :]


def post_norm(x, branch, g, b):
    return layer_norm(DN_ALPHA * x + branch, g, b)


def axial_rope(n_tokens, rot_dim):
    n_rows = n_tokens // GRID_W
    pos = jnp.arange(n_rows * GRID_W)
    row = (pos // GRID_W).astype(jnp.float32)
    col = (pos % GRID_W).astype(jnp.float32)
    n_freq = rot_dim // 4
    inv = 1.0 / (ROPE_BASE ** (jnp.arange(n_freq, dtype=jnp.float32) / n_freq))
    ang = jnp.concatenate([row[:, None] * inv, col[:, None] * inv], -1)
    return jnp.cos(ang), jnp.sin(ang)


def apply_rope(x, cos, sin):
    half = x.shape[-1] // 2
    shp = (cos.shape[0],) + (1,) * (x.ndim - 3) + (half,)
    c, s = cos.reshape(shp), sin.reshape(shp)
    xf = x.astype(jnp.float32)
    x1, x2 = xf[..., :half], xf[..., half:]
    return jnp.concatenate([x1 * c - x2 * s, x1 * s + x2 * c], -1).astype(x.dtype)


def _to_blocks(t):
    b, n = t.shape[:2]
    return jnp.moveaxis(t.reshape((b, n // Q_BLOCK, Q_BLOCK) + t.shape[2:]), 1, 0)


def _from_blocks(t):
    t = jnp.moveaxis(t, 0, 1)
    return t.reshape((t.shape[0], t.shape[1] * t.shape[2]) + t.shape[3:])


def over_query_blocks(fn, *qs):
    out = lax.map(lambda a: fn(*a), tuple(_to_blocks(t) for t in qs))
    return _from_blocks(out)


def sink_attention(q, k, v, sink):
    scale = HEAD_DIM ** -0.5

    def block(qb):
        s = jnp.einsum('bqgrd,bkgd->bgrqk', qb, k).astype(jnp.float32) * scale
        sk = jnp.broadcast_to(sink.astype(jnp.float32)[None, :, :, None, None], s.shape[:-1] + (1,))
        p = jax.nn.softmax(jnp.concatenate([s, sk], -1), axis=-1)[..., :-1]
        return jnp.einsum('bgrqk,bkgd->bqgrd', p.astype(v.dtype), v)

    return over_query_blocks(block, q)


def banded_sink_attention(q, k, v, k_ctx, v_ctx, sink):
    b, n = q.shape[:2]
    nb = n // WINDOW
    w3 = 3 * WINDOW
    s_c = k_ctx.shape[1]
    scale = HEAD_DIM ** -0.5
    qb = q.reshape(b, nb, WINDOW, A_KV_HEADS, A_GROUP, HEAD_DIM)

    def windows(t):
        tp = jnp.pad(t, ((0, 0), (WINDOW, WINDOW), (0, 0), (0, 0)))
        tp = tp.reshape(b, nb + 2, WINDOW, A_KV_HEADS, HEAD_DIM)
        return jnp.concatenate([tp[:, :-2], tp[:, 1:-1], tp[:, 2:]], axis=2)

    kw, vw = windows(k), windows(v)
    qpos = jnp.arange(nb)[:, None] * WINDOW + jnp.arange(WINDOW)[None, :]
    kpos = (jnp.arange(nb)[:, None] - 1) * WINDOW + jnp.arange(w3)[None, :]
    valid = ((jnp.abs(qpos[:, :, None] - kpos[:, None, :]) <= WINDOW)
             & (kpos[:, None, :] >= 0) & (kpos[:, None, :] < n))
    s_loc = jnp.einsum('bnqgrd,bnkgd->bngrqk', qb, kw).astype(jnp.float32) * scale
    s_loc = jnp.where(valid[None, :, None, None], s_loc, NEG_INF)
    s_ctx = jnp.einsum('bnqgrd,bkgd->bngrqk', qb, k_ctx).astype(jnp.float32) * scale
    s_sink = jnp.broadcast_to(sink.astype(jnp.float32)[None, None, :, :, None, None], s_loc.shape[:-1] + (1,))
    p = jax.nn.softmax(jnp.concatenate([s_loc, s_ctx, s_sink], -1), axis=-1).astype(v.dtype)
    o = (jnp.einsum('bngrqk,bnkgd->bnqgrd', p[..., :w3], vw)
         + jnp.einsum('bngrqk,bkgd->bnqgrd', p[..., w3:w3 + s_c], v_ctx))
    return o.reshape(b, n, A_KV_HEADS, A_GROUP, HEAD_DIM)


def diff_attention(q, k, v, lam):
    scale = B_HEAD_DIM ** -0.5

    def block(qb):
        s = jnp.einsum('bqmhd,bkmhd->bmhqk', qb, k).astype(jnp.float32) * scale
        p = jax.nn.softmax(s, axis=-1)
        a = p[:, 0] - lam * p[:, 1]
        return jnp.einsum('bhqk,bkhe->bqhe', a.astype(v.dtype), v)

    return over_query_blocks(block, q)


def mla_attention(qn, qr, kn, kr, v):
    scale = (C_NOPE + C_ROPE) ** -0.5

    def block(qnb, qrb):
        s = (jnp.einsum('bqhd,bkhd->bhqk', qnb, kn)
             + jnp.einsum('bqhd,bkd->bhqk', qrb, kr)).astype(jnp.float32) * scale
        p = jax.nn.softmax(s, axis=-1)
        return jnp.einsum('bhqk,bkhd->bqhd', p.astype(v.dtype), v)

    return over_query_blocks(block, qn, qr)


def project(h, w_in):
    b, n = h.shape[:2]
    z = h @ w_in
    idx = []
    acc = 0
    for s in IN_SIZES[:-1]:
        acc += s
        idx.append(acc)
    qa, ka, va, qb, kb, vb, cq, ckv, kr, g = jnp.split(z, idx, axis=-1)
    return dict(
        qa=qa.reshape(b, n, A_KV_HEADS, A_GROUP, HEAD_DIM),
        ka=ka.reshape(b, n, A_KV_HEADS, HEAD_DIM),
        va=va.reshape(b, n, A_KV_HEADS, HEAD_DIM),
        qb=qb.reshape(b, n, 2, B_HEADS, B_HEAD_DIM),
        kb=kb.reshape(b, n, 2, B_HEADS, B_HEAD_DIM),
        vb=vb.reshape(b, n, B_HEADS, B_V_DIM),
        cq=cq, ckv=ckv, kr=kr,
        gates=g.reshape(b, n, N_BRANCH, D_MODEL))


def mla_q(cq, q_norm, w_uq):
    q = (rms_norm(cq, q_norm) @ w_uq).reshape(cq.shape[:2] + (C_HEADS, C_NOPE + C_ROPE))
    return q[..., :C_NOPE], q[..., C_NOPE:]


def mla_kv(ckv_n, w_ukv):
    kv = (ckv_n @ w_ukv).reshape(ckv_n.shape[:2] + (C_HEADS, C_NOPE + C_VDIM))
    return kv[..., :C_NOPE], kv[..., C_NOPE:]


def diff_lambda(lq1, lk1, lq2, lk2, lam_init):
    f = jnp.float32
    return (jnp.exp(jnp.sum(lq1.astype(f) * lk1.astype(f)))
            - jnp.exp(jnp.sum(lq2.astype(f) * lk2.astype(f))) + lam_init)


def diff_out(o, subln, lam_init):
    return rms_norm(o, subln) * (1.0 - lam_init)


def merge_branches(oa, ob, oc, gates, lp):
    b, n = oa.shape[:2]
    ya = oa.reshape(b, n, -1) @ lp['w_br_a']
    yb = ob.reshape(b, n, -1) @ lp['w_br_b']
    yc = oc.reshape(b, n, -1) @ lp['w_br_c']
    gs = jax.nn.sigmoid(gates.astype(jnp.float32)).astype(oa.dtype)
    mixed = gs[:, :, 0] * ya + gs[:, :, 1] * yb + gs[:, :, 2] * yc
    return mixed @ lp['w_out']


def attn_context(h, lp, lam, lam_init):
    p = project(h, lp['w_in'])
    oa = sink_attention(p['qa'], p['ka'], p['va'], lp['sink'])
    ob = diff_out(diff_attention(p['qb'], p['kb'], p['vb'], lam), lp['subln'], lam_init)
    ckv_n = rms_norm(p['ckv'], lp['kv_norm'])
    qn, qr = mla_q(p['cq'], lp['q_norm'], lp['w_uq'])
    kn, vc = mla_kv(ckv_n, lp['w_ukv'])
    oc = mla_attention(qn, qr, kn, p['kr'], vc)
    out = merge_branches(oa, ob, oc, p['gates'], lp)
    return out, (p['ka'], p['va'], p['kb'], p['vb'], ckv_n, p['kr'])


def attn_latent(h, ctx, lp, lam, lam_init):
    ka_c, va_c, kb_c, vb_c, ckv_c, kr_c = ctx
    n = h.shape[1]
    p = project(h, lp['w_in'])
    cos_a, sin_a = axial_rope(n, HEAD_DIM)
    oa = banded_sink_attention(apply_rope(p['qa'], cos_a, sin_a), apply_rope(p['ka'], cos_a, sin_a),
                               p['va'], ka_c, va_c, lp['sink'])
    cos_b, sin_b = axial_rope(n, B_HEAD_DIM)
    kb_all = jnp.concatenate([kb_c, apply_rope(p['kb'], cos_b, sin_b)], axis=1)
    vb_all = jnp.concatenate([vb_c, p['vb']], axis=1)
    ob = diff_out(diff_attention(apply_rope(p['qb'], cos_b, sin_b), kb_all, vb_all, lam), lp['subln'], lam_init)
    cos_c, sin_c = axial_rope(n, C_ROPE)
    qn, qr = mla_q(p['cq'], lp['q_norm'], lp['w_uq'])
    qr = apply_rope(qr, cos_c, sin_c)
    ckv_all = jnp.concatenate([ckv_c, rms_norm(p['ckv'], lp['kv_norm'])], axis=1)
    kn, vc = mla_kv(ckv_all, lp['w_ukv'])
    kr_all = jnp.concatenate([kr_c, apply_rope(p['kr'], cos_c, sin_c)], axis=1)
    oc = mla_attention(qn, qr, kn, kr_all, vc)
    return merge_branches(oa, ob, oc, p['gates'], lp)


def clamped_swiglu(gu):
    g, u = gu[..., :D_FF], gu[..., D_FF:]
    g = jnp.minimum(g, SWIGLU_LIMIT)
    u = jnp.clip(u, -SWIGLU_LIMIT, SWIGLU_LIMIT)
    return g * jax.nn.sigmoid(SWIGLU_ALPHA * g) * (u + 1.0)


def moe(h, w_router, b_router, w_gu, b_gu, w_down, b_down):
    b, n, d = h.shape
    t = b * n
    x = h.reshape(t, d)
    logits = (x @ w_router + b_router).astype(jnp.float32)
    top_val, top_idx = lax.top_k(logits, TOP_K)
    gate = jax.nn.softmax(top_val, axis=-1).astype(h.dtype)
    n_slot = t * TOP_K
    e_flat = top_idx.reshape(-1).astype(jnp.int32)
    tok_flat = jnp.arange(n_slot, dtype=jnp.int32) // TOP_K
    order = jnp.argsort(e_flat)
    e_sorted = e_flat[order]
    sizes = jnp.zeros((N_EXPERTS,), jnp.int32).at[e_flat].add(1)
    padded = (sizes + MOE_BLOCK - 1) // MOE_BLOCK * MOE_BLOCK
    start = jnp.cumsum(sizes) - sizes
    pend = jnp.cumsum(padded)
    pstart = pend - padded
    dest_sorted = pstart[e_sorted] + jnp.arange(n_slot, dtype=jnp.int32) - start[e_sorted]
    cap = n_slot + N_EXPERTS * MOE_BLOCK
    n_blk = cap // MOE_BLOCK
    buf_tok = jnp.full((cap,), t, jnp.int32).at[dest_sorted].set(tok_flat[order])
    x_buf = jnp.concatenate([x, jnp.zeros((1, d), x.dtype)], 0)[buf_tok]
    blk_e = jnp.minimum(jnp.searchsorted(pend, jnp.arange(n_blk, dtype=jnp.int32) * MOE_BLOCK, side='right'),
                        N_EXPERTS - 1)

    def expert_block(args):
        xb, e = args
        return clamped_swiglu(xb @ w_gu[e] + b_gu[e]) @ w_down[e] + b_down[e]

    y_buf = lax.map(expert_block, (x_buf.reshape(n_blk, MOE_BLOCK, d), blk_e)).reshape(cap, d)
    dest = jnp.zeros((n_slot,), jnp.int32).at[order].set(dest_sorted)
    y_slot = y_buf[dest].reshape(t, TOP_K, d)
    return jnp.einsum('tk,tkd->td', gate, y_slot).reshape(b, n, d)


def setup_inputs(seed: int = 0) -> dict:
    key = jax.random.key(seed)
    keys = jax.random.split(key, 48)
    cnt = [0]
    f32 = jnp.float32

    def nrm(shape, scale):
        k = keys[cnt[0]]
        cnt[0] += 1
        return jax.random.normal(k, shape, f32) * scale

    D = D_MODEL
    L = PAST_LEN
    out = {}
    out['x_prompt'] = nrm((BATCH, SEQ, D), 1.0)
    out['x_sample'] = nrm((DEC_BATCH, DEC_SEQ, D), 1.0)
    out['c'] = nrm((DEC_BATCH, D), 1.0)
    out['cache_a_k'] = nrm((DEC_BATCH, DEPTH, L, A_KV_HEADS, HEAD_DIM), 1.0)
    out['cache_a_v'] = nrm((DEC_BATCH, DEPTH, L, A_KV_HEADS, HEAD_DIM), 1.0)
    out['cache_b_k'] = nrm((DEC_BATCH, DEPTH, L, 2, B_HEADS, B_HEAD_DIM), 1.0)
    out['cache_b_v'] = nrm((DEC_BATCH, DEPTH, L, B_HEADS, B_V_DIM), 1.0)
    out['cache_c_kv'] = nrm((DEC_BATCH, DEPTH, L, C_KV_LORA), 1.0)
    out['cache_c_kr'] = nrm((DEC_BATCH, DEPTH, L, C_ROPE), 1.0)
    out['c_ctx'] = nrm((D,), 1.0)
    out['w_ada'] = nrm((DEPTH, D, 6 * D), 0.5 * D ** -0.5)
    out['b_ada'] = nrm((DEPTH, 6 * D), 0.02)
    out['w_in'] = nrm((DEPTH, D, D_IN), D ** -0.5)
    out['sink_a'] = nrm((DEPTH, A_HEADS), 0.5)
    out['lam_q1'] = nrm((DEPTH, B_HEAD_DIM), 0.1)
    out['lam_k1'] = nrm((DEPTH, B_HEAD_DIM), 0.1)
    out['lam_q2'] = nrm((DEPTH, B_HEAD_DIM), 0.1)
    out['lam_k2'] = nrm((DEPTH, B_HEAD_DIM), 0.1)
    out['subln_b'] = 1.0 + nrm((DEPTH, B_V_DIM), 0.02)
    out['q_norm_c'] = 1.0 + nrm((DEPTH, C_Q_LORA), 0.02)
    out['kv_norm_c'] = 1.0 + nrm((DEPTH, C_KV_LORA), 0.02)
    out['w_uq'] = nrm((DEPTH, C_Q_LORA, C_HEADS * (C_NOPE + C_ROPE)), C_Q_LORA ** -0.5)
    out['w_ukv'] = nrm((DEPTH, C_KV_LORA, C_HEADS * (C_NOPE + C_VDIM)), C_KV_LORA ** -0.5)
    out['w_br_a'] = nrm((DEPTH, A_HEADS * HEAD_DIM, D), (A_HEADS * HEAD_DIM) ** -0.5)
    out['w_br_b'] = nrm((DEPTH, B_HEADS * B_V_DIM, D), (B_HEADS * B_V_DIM) ** -0.5)
    out['w_br_c'] = nrm((DEPTH, C_HEADS * C_VDIM, D), (C_HEADS * C_VDIM) ** -0.5)
    out['w_out'] = nrm((DEPTH, D, D), DN_BETA * D ** -0.5)
    out['ln1_g'] = 1.0 + nrm((DEPTH, D), 0.02)
    out['ln1_b'] = nrm((DEPTH, D), 0.02)
    out['ln2_g'] = 1.0 + nrm((DEPTH, D), 0.02)
    out['ln2_b'] = nrm((DEPTH, D), 0.02)
    out['w_router'] = nrm((DEPTH, D, N_EXPERTS), D ** -0.5)
    out['b_router'] = nrm((DEPTH, N_EXPERTS), 0.01)
    out['w_gu'] = nrm((DEPTH, N_EXPERTS, D, 2 * D_FF), D ** -0.5)
    out['b_gu'] = nrm((DEPTH, N_EXPERTS, 2 * D_FF), 0.01)
    out['w_down'] = nrm((DEPTH, N_EXPERTS, D_FF, D), DN_BETA * D_FF ** -0.5)
    out['b_down'] = nrm((DEPTH, N_EXPERTS, D), 0.01)
    return out


def reference(x_prompt, x_sample, c, cache_a_k, cache_a_v, cache_b_k, cache_b_v, cache_c_kv, cache_c_kr,
              c_ctx, w_ada, b_ada, w_in, sink_a, lam_q1, lam_k1, lam_q2, lam_k2, subln_b, q_norm_c, kv_norm_c,
              w_uq, w_ukv, w_br_a, w_br_b, w_br_c, w_out, ln1_g, ln1_b, ln2_g, ln2_b,
              w_router, b_router, w_gu, b_gu, w_down, b_down):
    xp, xs = x_prompt, x_sample
    st_a_k, st_a_v, st_b_k, st_b_v, st_c_kv, st_c_kr = [], [], [], [], [], []
    for l in range(DEPTH):
        lam_init = 0.8 - 0.6 * math.exp(-0.3 * l)
        lp = dict(w_in=w_in[l], sink=sink_a[l].reshape(A_KV_HEADS, A_GROUP), subln=subln_b[l],
                  q_norm=q_norm_c[l], kv_norm=kv_norm_c[l], w_uq=w_uq[l], w_ukv=w_ukv[l],
                  w_br_a=w_br_a[l], w_br_b=w_br_b[l], w_br_c=w_br_c[l], w_out=w_out[l])
        lam = diff_lambda(lam_q1[l], lam_k1[l], lam_q2[l], lam_k2[l], lam_init)
        m_ctx = adaln(c_ctx[None, :], w_ada[l], b_ada[l])
        m_lat = adaln(c, w_ada[l], b_ada[l])

        a_out, ctx_t = attn_context(modulate(xp, m_ctx, 0), lp, lam, lam_init)
        xp = post_norm(xp, m_ctx[:, 2][:, None, :] * a_out, ln1_g[l], ln1_b[l])
        f_out = moe(modulate(xp, m_ctx, 3), w_router[l], b_router[l], w_gu[l], b_gu[l], w_down[l], b_down[l])
        xp = post_norm(xp, m_ctx[:, 5][:, None, :] * f_out, ln2_g[l], ln2_b[l])
        st_a_k.append(ctx_t[0])
        st_a_v.append(ctx_t[1])
        st_b_k.append(ctx_t[2])
        st_b_v.append(ctx_t[3])
        st_c_kv.append(ctx_t[4])
        st_c_kr.append(ctx_t[5])

        ctx_l = (cache_a_k[:, l], cache_a_v[:, l], cache_b_k[:, l], cache_b_v[:, l], cache_c_kv[:, l], cache_c_kr[:, l])
        a_out = attn_latent(modulate(xs, m_lat, 0), ctx_l, lp, lam, lam_init)
        xs = post_norm(xs, m_lat[:, 2][:, None, :] * a_out, ln1_g[l], ln1_b[l])
        f_out = moe(modulate(xs, m_lat, 3), w_router[l], b_router[l], w_gu[l], b_gu[l], w_down[l], b_down[l])
        xs = post_norm(xs, m_lat[:, 5][:, None, :] * f_out, ln2_g[l], ln2_b[l])

    return (xp, xs, jnp.stack(st_a_k, axis=1), jnp.stack(st_a_v, axis=1), jnp.stack(st_b_k, axis=1),
            jnp.stack(st_b_v, axis=1), jnp.stack(st_c_kv, axis=1), jnp.stack(st_c_kr, axis=1))
```

```python
import functools
import math

import jax
import jax.numpy as jnp
from jax import lax
from jax.experimental import pallas as pl
from jax.experimental.pallas import tpu as pltpu

F32 = jnp.float32
BF16 = jnp.bfloat16

D_MODEL = 2048
GRID_W = 64
WINDOW = 128
A_HEADS = 8
A_KV_HEADS = 2
A_GROUP = A_HEADS // A_KV_HEADS
HEAD_DIM = 128
B_HEADS = 4
B_HEAD_DIM = 128
B_V_DIM = 2 * B_HEAD_DIM
C_HEADS = 8
C_NOPE = 128
C_ROPE = 64
C_VDIM = 128
C_Q_LORA = 512
C_KV_LORA = 256
C_QK_PAD = 256
N_BRANCH = 3
N_EXPERTS = 32
TOP_K = 4
D_FF = 2048
SWIGLU_LIMIT = 7.0
SWIGLU_ALPHA = 1.702
ROPE_BASE = 10000.0
LN_EPS = 1e-5
RMS_EPS = 1e-6
NEG_INF = -1e30

COL_QA = 0
COL_KA = 1024
COL_VA = 1280
COL_QB = 1536
COL_KB = 2560
COL_VB = 3584
COL_CQ = 4608
COL_CKV = 5120
COL_KR = 5376
N_MAIN = 5632
N_GATES = N_BRANCH * D_MODEL
N_IN_SRC = COL_KR + C_ROPE

LANE = 128
V7X_VMEM_LIMIT = 56 * 1024 * 1024

PROJ_TN = 512
MERGE_TM = 256
MOE_TB = 256
MOE_FF = 512
ROUTER_TM = 256


def _cparams(n_grid):
    return pltpu.CompilerParams(dimension_semantics=("arbitrary",) * n_grid, vmem_limit_bytes=V7X_VMEM_LIMIT)


def _sigmoid(x):
    return 1.0 / (1.0 + jnp.exp(-x))


def _layer_norm_rows(x):
    mu = jnp.mean(x, axis=-1, keepdims=True)
    xc = x - mu
    var = jnp.mean(xc * xc, axis=-1, keepdims=True)
    return xc * lax.rsqrt(var + LN_EPS)


def _rms_rows(x, g):
    return x * lax.rsqrt(jnp.mean(x * x, axis=-1, keepdims=True) + RMS_EPS) * g


def _rope_half128(x, cosf, sinf):
    return x * cosf + pltpu.roll(x, 64, 1) * sinf


def _rope_pad64(x, cf, sa, sb):
    return x * cf + pltpu.roll(x, 96, 1) * sa + pltpu.roll(x, 32, 1) * sb


def _adaln_kernel(c_ref, w_ref, b_ref, o_ref):
    c = c_ref[...]
    s = (c * _sigmoid(c)).astype(BF16)
    o_ref[...] = jnp.dot(s, w_ref[...].astype(BF16), preferred_element_type=F32) + b_ref[...]


def adaln_all(cond8, w_ada, b_ada):
    depth, d, n = w_ada.shape
    tn = 1024
    return pl.pallas_call(
        _adaln_kernel,
        grid=(depth, n // tn),
        in_specs=[
            pl.BlockSpec((8, d), lambda l, j: (0, 0)),
            pl.BlockSpec((None, d, tn), lambda l, j: (l, 0, j)),
            pl.BlockSpec((None, 1, tn), lambda l, j: (l, 0, j)),
        ],
        out_specs=pl.BlockSpec((None, 8, tn), lambda l, j: (l, 0, j)),
        out_shape=jax.ShapeDtypeStruct((depth, 8, n), F32),
        compiler_params=_cparams(2),
        name="adaln",
    )(cond8, w_ada, b_ada.reshape(depth, 1, n))


def _mod_spec(layer, slot, row_of_tile):
    return pl.BlockSpec((None, None, None, 1, D_MODEL), lambda *g: (layer, row_of_tile(*g), slot, 0, 0))


ROPE_FULL_TILES = (0, 1, 3, 4, 5, 6)
ROPE_HALF_TILE = 2


def _lnmm_kernel(*refs, mode):
    if mode == "rope":
        x_ref, sc_ref, sh_ref, w_ref, cos_ref, sin_ref, o_ref, h_scr = refs
    else:
        x_ref, sc_ref, sh_ref, w_ref, o_ref, h_scr = refs
    j = pl.program_id(1)

    @pl.when(j == 0)
    def _():
        h = _layer_norm_rows(x_ref[...]) * (1.0 + sc_ref[...]) + sh_ref[...]
        h_scr[...] = h.astype(BF16)

    acc = jnp.dot(h_scr[...], w_ref[...], preferred_element_type=F32)
    if mode == "sigmoid":
        o_ref[...] = _sigmoid(acc)
    elif mode == "plain":
        o_ref[...] = acc
    else:
        is_full = functools.reduce(jnp.logical_or, [j == t for t in ROPE_FULL_TILES])
        is_half = j == ROPE_HALF_TILE

        def rotated(n_heads):
            cosf, sinf = cos_ref[...], sin_ref[...]
            for hh in range(PROJ_TN // HEAD_DIM):
                blk = acc[:, hh * HEAD_DIM:(hh + 1) * HEAD_DIM]
                if hh < n_heads:
                    blk = _rope_half128(blk, cosf, sinf)
                o_ref[:, hh * HEAD_DIM:(hh + 1) * HEAD_DIM] = blk

        @pl.when(is_full)
        def _():
            rotated(PROJ_TN // HEAD_DIM)

        @pl.when(is_half)
        def _():
            rotated(A_KV_HEADS)

        @pl.when(jnp.logical_not(jnp.logical_or(is_full, is_half)))
        def _():
            o_ref[...] = acc


def ln_mod_matmul(x, mods, layer, slot, rows_per_cond, cond_row0, w, *, mode, rope=None, n_seq=None):
    rows, d = x.shape
    n = w.shape[1]
    tm = min(512, n_seq)
    tn = PROJ_TN
    row_of = lambda i, j: cond_row0 + (i * tm) // rows_per_cond
    in_specs = [
        pl.BlockSpec((tm, d), lambda i, j: (i, 0)),
        _mod_spec(layer, slot + 1, row_of),
        _mod_spec(layer, slot, row_of),
        pl.BlockSpec((d, tn), lambda i, j: (0, j)),
    ]
    args = [x, mods, mods, w]
    if mode == "rope":
        per_seq = n_seq // tm
        in_specs += [pl.BlockSpec((tm, LANE), lambda i, j: (i % per_seq, 0))] * 2
        args += list(rope)
    return pl.pallas_call(
        functools.partial(_lnmm_kernel, mode=mode),
        grid=(rows // tm, n // tn),
        in_specs=in_specs,
        out_specs=pl.BlockSpec((tm, tn), lambda i, j: (i, j)),
        out_shape=jax.ShapeDtypeStruct((rows, n), F32),
        scratch_shapes=[pltpu.VMEM((tm, d), BF16)],
        compiler_params=_cparams(2),
        name="ln_mod_matmul_" + mode,
    )(*args)


def _attn_kernel(*refs, n_maps, n_seg, scale, use_sink, diff, lam_init):
    pos = 0
    if use_sink:
        sink_ref = refs[0]
        pos = 1
    q_refs = refs[pos:pos + n_maps]
    pos += n_maps
    k_refs = [refs[pos + sg * n_maps: pos + (sg + 1) * n_maps] for sg in range(n_seg)]
    pos += n_seg * n_maps
    v_refs = refs[pos:pos + n_seg]
    pos += n_seg
    if diff:
        lq1, lk1, lq2, lk2, subln_ref = refs[pos:pos + 5]
        pos += 5
    o_ref = refs[pos]

    probs = []
    for m in range(n_maps):
        q = q_refs[m][...].astype(BF16)
        s = [lax.dot_general(q, k_refs[sg][m][...].astype(BF16), (((1,), (1,)), ((), ())),
                             preferred_element_type=F32) * scale for sg in range(n_seg)]
        mx = functools.reduce(jnp.maximum, [jnp.max(t, axis=-1, keepdims=True) for t in s])
        if use_sink:
            sink = sink_ref[pl.program_id(1)]
            mx = jnp.maximum(mx, sink)
        e = [jnp.exp(t - mx) for t in s]
        den = functools.reduce(jnp.add, [jnp.sum(t, axis=-1, keepdims=True) for t in e])
        if use_sink:
            den = den + jnp.exp(sink - mx)
        inv = 1.0 / den
        probs.append([t * inv for t in e])
    if diff:
        f = jnp.sum(lq1[...] * lk1[...], axis=-1, keepdims=True)
        g = jnp.sum(lq2[...] * lk2[...], axis=-1, keepdims=True)
        lam = jnp.exp(f) - jnp.exp(g) + lam_init
        a = [probs[0][sg] - lam * probs[1][sg] for sg in range(n_seg)]
    else:
        a = probs[0]
    o = functools.reduce(jnp.add, [jnp.dot(a[sg].astype(BF16), v_refs[sg][...].astype(BF16),
                                           preferred_element_type=F32) for sg in range(n_seg)])
    if diff:
        o = _rms_rows(o, subln_ref[...]) * (1.0 - lam_init)
    o_ref[...] = o.astype(o_ref.dtype)


def attention(grid, q_ops, k_ops, v_ops, out_spec, out_shape, *, scale, sink=None, diff_ops=None, lam_init=0.0,
              name="attn"):
    n_maps, n_seg = len(q_ops), len(k_ops)
    ops = []
    if sink is not None:
        ops.append((sink, pl.BlockSpec(memory_space=pltpu.SMEM)))
    ops += list(q_ops)
    for seg in k_ops:
        ops += list(seg)
    ops += list(v_ops)
    if diff_ops is not None:
        ops += list(diff_ops)
    kern = functools.partial(_attn_kernel, n_maps=n_maps, n_seg=n_seg, scale=scale, use_sink=sink is not None,
                             diff=diff_ops is not None, lam_init=lam_init)
    return pl.pallas_call(
        kern, grid=grid,
        in_specs=[s for _, s in ops],
        out_specs=out_spec, out_shape=out_shape,
        compiler_params=_cparams(len(grid)), name=name,
    )(*[a for a, _ in ops])


def _band_kernel(sink_ref, q_ref, kp_ref, kc_ref, kn_ref, kx_ref, vp_ref, vc_ref, vn_ref, vx_ref, o_ref, *, n_blocks):
    g = pl.program_id(1)
    i = pl.program_id(2)
    w = WINDOW
    q = q_ref[...]
    qs = jnp.concatenate([q[:, r * HEAD_DIM:(r + 1) * HEAD_DIM] for r in range(A_GROUP)], axis=0).astype(BF16)
    kcat = jnp.concatenate([kp_ref[...], kc_ref[...], kn_ref[...], kx_ref[...]], axis=0).astype(BF16)
    vcat = jnp.concatenate([vp_ref[...], vc_ref[...], vn_ref[...], vx_ref[...]], axis=0).astype(BF16)
    s = lax.dot_general(qs, kcat, (((1,), (1,)), ((), ())), preferred_element_type=F32) * (HEAD_DIM ** -0.5)
    n_keys = s.shape[1]
    row = lax.broadcasted_iota(jnp.int32, s.shape, 0) & (w - 1)
    col = lax.broadcasted_iota(jnp.int32, s.shape, 1)
    prev_slack = jnp.where(i > 0, col - row, -1)
    next_slack = jnp.where(i < n_blocks - 1, row - col + 2 * w, -1)
    slack = jnp.where(col < w, prev_slack, jnp.where(col < 2 * w, 0, jnp.where(col < 3 * w, next_slack, 0)))
    s = jnp.where(slack >= 0, s, NEG_INF)
    sink = jnp.concatenate([jnp.full((w, 1), sink_ref[g * A_GROUP + r], F32) for r in range(A_GROUP)], axis=0)
    mx = jnp.maximum(jnp.max(s, axis=-1, keepdims=True), sink)
    e = jnp.exp(s - mx)
    den = jnp.sum(e, axis=-1, keepdims=True) + jnp.exp(sink - mx)
    p = (e * (1.0 / den)).astype(BF16)
    o = jnp.dot(p, vcat, preferred_element_type=F32)
    for r in range(A_GROUP):
        o_ref[:, r * HEAD_DIM:(r + 1) * HEAD_DIM] = o[r * w:(r + 1) * w].astype(o_ref.dtype)


def banded_attention(z, cache_k, cache_v, layer, sink, n_batch, n_seq):
    nb = n_seq // WINDOW
    l_ctx = cache_k.shape[2]
    qw = A_GROUP * HEAD_DIM
    kcol, vcol = COL_KA // HEAD_DIM, COL_VA // HEAD_DIM

    def blk(col0, shift):
        def imap(b, g, i):
            return (b * nb + jnp.clip(i + shift, 0, nb - 1), col0 + g)
        return pl.BlockSpec((WINDOW, HEAD_DIM), imap)

    ctx_spec = pl.BlockSpec((None, None, l_ctx, HEAD_DIM), lambda b, g, i: (b, layer, 0, g))
    return pl.pallas_call(
        functools.partial(_band_kernel, n_blocks=nb),
        grid=(n_batch, A_KV_HEADS, nb),
        in_specs=[pl.BlockSpec(memory_space=pltpu.SMEM),
                  pl.BlockSpec((WINDOW, qw), lambda b, g, i: (b * nb + i, g)),
                  blk(kcol, -1), blk(kcol, 0), blk(kcol, 1), ctx_spec,
                  blk(vcol, -1), blk(vcol, 0), blk(vcol, 1), ctx_spec],
        out_specs=pl.BlockSpec((WINDOW, qw), lambda b, g, i: (b * nb + i, g)),
        out_shape=jax.ShapeDtypeStruct((n_batch * n_seq, A_HEADS * HEAD_DIM), BF16),
        compiler_params=_cparams(3), name="banded_attn",
    )(sink, z, z, z, z, cache_k, z, z, z, cache_v)


def _cq_kernel(*refs, rope):
    if rope:
        cq_ref, g_ref, w_ref, cf_ref, sa_ref, sb_ref, o_ref = refs
    else:
        cq_ref, g_ref, w_ref, o_ref = refs
    y = _rms_rows(cq_ref[...], g_ref[...]).astype(BF16)
    q = jnp.dot(y, w_ref[...], preferred_element_type=F32)
    if rope:
        cf, sa, sb = cf_ref[...], sa_ref[...], sb_ref[...]
    for h in range(C_HEADS):
        lo = h * C_QK_PAD
        o_ref[:, lo:lo + C_NOPE] = q[:, lo:lo + C_NOPE].astype(o_ref.dtype)
        blk = q[:, lo + C_NOPE:lo + C_QK_PAD]
        if rope:
            blk = _rope_pad64(blk, cf, sa, sb)
        o_ref[:, lo + C_NOPE:lo + C_QK_PAD] = blk.astype(o_ref.dtype)


def c_query(z, q_norm, w_uq_p, rope_c, n_seq):
    rows = z.shape[0]
    tm = min(512, n_seq)
    rope = rope_c is not None
    in_specs = [pl.BlockSpec((tm, C_Q_LORA), lambda i: (i, COL_CQ // C_Q_LORA)),
                pl.BlockSpec((1, C_Q_LORA), lambda i: (0, 0)),
                pl.BlockSpec(w_uq_p.shape, lambda i: (0, 0))]
    args = [z, q_norm, w_uq_p]
    if rope:
        per_seq = n_seq // tm
        in_specs += [pl.BlockSpec((tm, LANE), lambda i: (i % per_seq, 0))] * 3
        args += list(rope_c)
    return pl.pallas_call(
        functools.partial(_cq_kernel, rope=rope),
        grid=(rows // tm,), in_specs=in_specs,
        out_specs=pl.BlockSpec((tm, C_HEADS * C_QK_PAD), lambda i: (i, 0)),
        out_shape=jax.ShapeDtypeStruct((rows, C_HEADS * C_QK_PAD), BF16),
        compiler_params=_cparams(1), name="c_query",
    )(*args)


def _ckv_kernel(*refs, normalize, rope, emit_norm):
    refs = list(refs)
    ckv_ref, kr_ref = refs[:2]
    pos = 2
    if normalize:
        g_ref = refs[pos]
        pos += 1
    wn_ref, wv_ref = refs[pos:pos + 2]
    pos += 2
    if rope:
        cf_ref, sa_ref, sb_ref = refs[pos:pos + 3]
        pos += 3
    k_ref, v_ref = refs[pos:pos + 2]
    pos += 2
    ckv = ckv_ref[...]
    if normalize:
        ckv = _rms_rows(ckv, g_ref[...])
    if emit_norm:
        refs[pos][...] = ckv
    cb = ckv.astype(BF16)
    kn = jnp.dot(cb, wn_ref[...], preferred_element_type=F32)
    v_ref[...] = jnp.dot(cb, wv_ref[...], preferred_element_type=F32).astype(v_ref.dtype)
    kr = kr_ref[...]
    if rope:
        kr = _rope_pad64(kr, cf_ref[...], sa_ref[...], sb_ref[...])
    kr = kr.astype(k_ref.dtype)
    for h in range(C_HEADS):
        lo = h * C_QK_PAD
        k_ref[:, lo:lo + C_NOPE] = kn[:, h * C_NOPE:(h + 1) * C_NOPE].astype(k_ref.dtype)
        k_ref[:, lo + C_NOPE:lo + C_QK_PAD] = kr


def c_keyvalue(ckv_op, kr_op, rows, tm, grid, kv_norm, w_nope, w_v, rope_c=None, rope_spec=None, emit_norm=False):
    ops = [ckv_op, kr_op]
    full = lambda a: (a, pl.BlockSpec(a.shape, lambda *g: (0,) * a.ndim))
    if kv_norm is not None:
        ops.append(full(kv_norm))
    ops += [full(w_nope), full(w_v)]
    if rope_c is not None:
        ops += [(t, rope_spec) for t in rope_c]
    row_map = lambda *g: (functools.reduce(lambda a, b: a * grid[1] + b, g) if len(g) > 1 else g[0], 0)
    out_specs = [pl.BlockSpec((tm, C_HEADS * C_QK_PAD), row_map), pl.BlockSpec((tm, C_HEADS * C_VDIM), row_map)]
    out_shape = [jax.ShapeDtypeStruct((rows, C_HEADS * C_QK_PAD), BF16),
                 jax.ShapeDtypeStruct((rows, C_HEADS * C_VDIM), BF16)]
    if emit_norm:
        out_specs.append(pl.BlockSpec((tm, C_KV_LORA), row_map))
        out_shape.append(jax.ShapeDtypeStruct((rows, C_KV_LORA), F32))
    return pl.pallas_call(
        functools.partial(_ckv_kernel, normalize=kv_norm is not None, rope=rope_c is not None, emit_norm=emit_norm),
        grid=grid, in_specs=[s for _, s in ops], out_specs=out_specs, out_shape=out_shape,
        compiler_params=_cparams(len(grid)), name="c_keyvalue",
    )(*[a for a, _ in ops])


def _merge_kernel(oa_ref, ob_ref, oc_ref, g_ref, x_ref, ga_ref, lng_ref, lnb_ref, wa_ref, wb_ref, wc_ref, wo_ref,
                  o_ref, mix_scr, *, alpha):
    d = D_MODEL
    chunk = 512
    for c in range(d // chunk):
        sl = slice(c * chunk, (c + 1) * chunk)
        ya = jnp.dot(oa_ref[...], wa_ref[:, sl], preferred_element_type=F32)
        yb = jnp.dot(ob_ref[...], wb_ref[:, sl], preferred_element_type=F32)
        yc = jnp.dot(oc_ref[...], wc_ref[:, sl], preferred_element_type=F32)
        mixed = (g_ref[:, c * chunk:(c + 1) * chunk] * ya
                 + g_ref[:, d + c * chunk:d + (c + 1) * chunk] * yb
                 + g_ref[:, 2 * d + c * chunk:2 * d + (c + 1) * chunk] * yc)
        mix_scr[:, sl] = mixed.astype(BF16)
    out = jnp.dot(mix_scr[...], wo_ref[...], preferred_element_type=F32)
    y = alpha * x_ref[...] + ga_ref[...] * out
    o_ref[...] = _layer_norm_rows(y) * lng_ref[...] + lnb_ref[...]


def merge_block(oa, ob, oc, gates, x, mods, layer, rows_per_cond, cond_row0, ln_g, ln_b, wa, wb, wc, wo, alpha):
    rows, d = x.shape
    tm = MERGE_TM
    row_of = lambda i: cond_row0 + (i * tm) // rows_per_cond
    const = lambda a: pl.BlockSpec(a.shape, lambda i: (0, 0), pipeline_mode=pl.Buffered(1))
    rowblk = lambda a: pl.BlockSpec((tm, a.shape[1]), lambda i: (i, 0))
    vec = pl.BlockSpec((1, d), lambda i: (0, 0))
    return pl.pallas_call(
        functools.partial(_merge_kernel, alpha=alpha),
        grid=(rows // tm,),
        in_specs=[rowblk(oa), rowblk(ob), rowblk(oc), rowblk(gates), rowblk(x), _mod_spec(layer, 2, row_of),
                  vec, vec, const(wa), const(wb), const(wc), const(wo)],
        out_specs=pl.BlockSpec((tm, d), lambda i: (i, 0)),
        out_shape=jax.ShapeDtypeStruct((rows, d), F32),
        scratch_shapes=[pltpu.VMEM((tm, d), BF16)],
        compiler_params=_cparams(1), name="merge",
    )(oa, ob, oc, gates, x, mods, ln_g, ln_b, wa, wb, wc, wo)


ROUTE_RANK_LANE = TOP_K


def _router_kernel(x_ref, sc_ref, sh_ref, wr_ref, br_ref, h_ref, route_ref, gate_ref, cnt_ref, carry_scr):
    i = pl.program_id(0)

    @pl.when(i == 0)
    def _():
        carry_scr[...] = jnp.zeros_like(carry_scr)

    h = _layer_norm_rows(x_ref[...]) * (1.0 + sc_ref[...]) + sh_ref[...]
    h_ref[...] = h
    logits = jnp.dot(h, wr_ref[...], preferred_element_type=F32, precision=lax.Precision.HIGHEST) + br_ref[...]
    tm = logits.shape[0]
    lane = lax.broadcasted_iota(jnp.int32, (tm, LANE), 1)
    cur = jnp.where(lane < N_EXPERTS, logits, -jnp.inf)
    vals, idxs = [], []
    for _ in range(TOP_K):
        m = jnp.max(cur, axis=-1, keepdims=True)
        ik = jnp.min(jnp.where(cur == m, lane, LANE), axis=-1, keepdims=True)
        vals.append(m)
        idxs.append(ik)
        cur = jnp.where(lane == ik, -jnp.inf, cur)
    e = [jnp.exp(v - vals[0]) for v in vals]
    inv = 1.0 / functools.reduce(jnp.add, e)
    onehot = functools.reduce(jnp.add, [(lane == ik).astype(F32) for ik in idxs])
    r_i = lax.broadcasted_iota(jnp.int32, (tm, tm), 0)
    c_i = lax.broadcasted_iota(jnp.int32, (tm, tm), 1)
    tri = (c_i < r_i).astype(BF16)
    rank_full = jnp.dot(tri, onehot.astype(BF16), preferred_element_type=F32) + carry_scr[...]
    route = jnp.zeros((tm, LANE), jnp.int32)
    gate = jnp.zeros((tm, LANE), F32)
    for k in range(TOP_K):
        rk = jnp.sum(jnp.where(lane == idxs[k], rank_full, 0.0), axis=-1, keepdims=True).astype(jnp.int32)
        route = jnp.where(lane == k, idxs[k], route)
        route = jnp.where(lane == ROUTE_RANK_LANE + k, rk, route)
        gate = jnp.where(lane == k, e[k] * inv, gate)
    route_ref[...] = route
    gate_ref[...] = gate
    carry_scr[...] = carry_scr[...] + jnp.sum(onehot, axis=0, keepdims=True)
    cnt_ref[...] = carry_scr[...]


def router(x, mods, layer, rows_per_cond, cond_row0, w_router_p, b_router_p):
    rows, d = x.shape
    tm = ROUTER_TM
    row_of = lambda i: cond_row0 + (i * tm) // rows_per_cond
    rowblk = lambda w: pl.BlockSpec((tm, w), lambda i: (i, 0))
    return pl.pallas_call(
        _router_kernel, grid=(rows // tm,),
        in_specs=[rowblk(d), _mod_spec(layer, 4, row_of), _mod_spec(layer, 3, row_of),
                  pl.BlockSpec((d, LANE), lambda i: (0, 0)), pl.BlockSpec((1, LANE), lambda i: (0, 0))],
        out_specs=[rowblk(d), rowblk(LANE), rowblk(LANE), pl.BlockSpec((1, LANE), lambda i: (0, 0))],
        out_shape=[jax.ShapeDtypeStruct((rows, d), F32), jax.ShapeDtypeStruct((rows, LANE), jnp.int32),
                   jax.ShapeDtypeStruct((rows, LANE), F32), jax.ShapeDtypeStruct((1, LANE), F32)],
        scratch_shapes=[pltpu.VMEM((1, LANE), F32)],
        compiler_params=_cparams(1), name="router",
    )(x, mods, mods, w_router_p, b_router_p)


def _expert_kernel(blk_e, n_valid, active, buf_tok, buf_slot,
                   h_hbm, wg_ref, wu_ref, bg_ref, bu_ref, wd_ref, bd_ref, y_hbm,
                   xg_scr, xb_scr, acc_scr, sem_in, sem_out, *, n_ff):
    i = pl.program_id(0)
    j = pl.program_id(1)
    tb = xg_scr.shape[0]
    is_active = active[i] == 1
    nv = n_valid[i]
    base = i * tb

    def row_in(r):
        return pltpu.make_async_copy(h_hbm.at[pl.ds(buf_tok[base + r], 1)], xg_scr.at[pl.ds(r, 1)], sem_in)

    def row_out(r):
        return pltpu.make_async_copy(acc_scr.at[pl.ds(r, 1)], y_hbm.at[pl.ds(buf_slot[base + r], 1)], sem_out)

    def for_rows(fn):
        lax.fori_loop(0, nv, lambda r, c: (fn(r), c)[1], 0)

    @pl.when((i == 0) & (j == 0))
    def _():
        xg_scr[...] = jnp.zeros_like(xg_scr)

    @pl.when(is_active & (j == 0))
    def _():
        for_rows(lambda r: row_in(r).start())
        for_rows(lambda r: row_in(r).wait())
        xb_scr[...] = xg_scr[...].astype(BF16)

    @pl.when(is_active)
    def _():
        xb = xb_scr[...]
        g = jnp.dot(xb, wg_ref[...], preferred_element_type=F32) + bg_ref[...]
        u = jnp.dot(xb, wu_ref[...], preferred_element_type=F32) + bu_ref[...]
        g = jnp.minimum(g, SWIGLU_LIMIT)
        u = jnp.clip(u, -SWIGLU_LIMIT, SWIGLU_LIMIT)
        a = (g * _sigmoid(SWIGLU_ALPHA * g) * (u + 1.0)).astype(BF16)
        contrib = jnp.dot(a, wd_ref[...], preferred_element_type=F32)

        @pl.when(j == 0)
        def _():
            acc_scr[...] = contrib + bd_ref[...]

        @pl.when(j > 0)
        def _():
            acc_scr[...] = acc_scr[...] + contrib

    @pl.when(is_active & (j == n_ff - 1))
    def _():
        for_rows(lambda r: row_out(r).start())
        for_rows(lambda r: row_out(r).wait())


def expert_ffn(h, meta, layer, w_gu, b_gu, w_down, b_down):
    t, d = h.shape
    blk_e, n_valid, active, buf_tok, buf_slot = meta
    n_blk = blk_e.shape[0]
    tb, ff = MOE_TB, MOE_FF
    n_ff = D_FF // ff
    jj = lambda i, j, act: jnp.where(act[i] == 1, j, n_ff - 1)
    grid_spec = pltpu.PrefetchScalarGridSpec(
        num_scalar_prefetch=5,
        grid=(n_blk, n_ff),
        in_specs=[
            pl.BlockSpec(memory_space=pl.ANY),
            pl.BlockSpec((None, None, d, ff), lambda i, j, be, nv, act, bt, bs: (layer, be[i], 0, jj(i, j, act))),
            pl.BlockSpec((None, None, d, ff), lambda i, j, be, nv, act, bt, bs: (layer, be[i], 0, n_ff + jj(i, j, act))),
            pl.BlockSpec((None, None, 1, ff), lambda i, j, be, nv, act, bt, bs: (layer, be[i], 0, jj(i, j, act))),
            pl.BlockSpec((None, None, 1, ff), lambda i, j, be, nv, act, bt, bs: (layer, be[i], 0, n_ff + jj(i, j, act))),
            pl.BlockSpec((None, None, ff, d), lambda i, j, be, nv, act, bt, bs: (layer, be[i], jj(i, j, act), 0)),
            pl.BlockSpec((None, None, 1, d), lambda i, j, be, nv, act, bt, bs: (layer, be[i], 0, 0)),
        ],
        out_specs=pl.BlockSpec(memory_space=pl.ANY),
        scratch_shapes=[pltpu.VMEM((tb, d), F32), pltpu.VMEM((tb, d), BF16), pltpu.VMEM((tb, d), F32),
                        pltpu.SemaphoreType.DMA(()), pltpu.SemaphoreType.DMA(())],
    )
    return pl.pallas_call(
        functools.partial(_expert_kernel, n_ff=n_ff),
        grid_spec=grid_spec,
        out_shape=jax.ShapeDtypeStruct((TOP_K * t, d), F32),
        compiler_params=_cparams(2), name="expert_ffn",
    )(blk_e, n_valid, active, buf_tok, buf_slot, h, w_gu, w_gu, b_gu, b_gu, w_down, b_down)


def routing_metadata(route, cnt, t):
    tb = MOE_TB
    idx = route[:, :TOP_K]
    rank = route[:, ROUTE_RANK_LANE:ROUTE_RANK_LANE + TOP_K]
    sizes = cnt[0, :N_EXPERTS].astype(jnp.int32)
    padded = (sizes + tb - 1) // tb * tb
    pend = jnp.cumsum(padded)
    pstart = pend - padded
    dest = (pstart[idx] + rank).reshape(-1)
    cap = t * TOP_K + N_EXPERTS * tb
    n_blk = cap // tb
    tok = jnp.arange(t * TOP_K, dtype=jnp.int32) // TOP_K
    slot = (jnp.arange(t * TOP_K, dtype=jnp.int32) % TOP_K) * t + tok
    buf_tok = jnp.zeros((cap,), jnp.int32).at[dest].set(tok)
    buf_slot = jnp.zeros((cap,), jnp.int32).at[dest].set(slot)
    blk_start = jnp.arange(n_blk, dtype=jnp.int32) * tb
    n_act = pend[-1] // tb
    blk_e = jnp.minimum(jnp.searchsorted(pend, blk_start, side='right'), N_EXPERTS - 1).astype(jnp.int32)
    active = (jnp.arange(n_blk) < n_act).astype(jnp.int32)
    blk_e = jnp.where(active == 1, blk_e, blk_e[jnp.maximum(n_act - 1, 0)])
    n_valid = jnp.clip(pstart[blk_e] + sizes[blk_e] - blk_start, 0, tb) * active
    return blk_e, n_valid.astype(jnp.int32), active, buf_tok, buf_slot


def _combine_kernel(y_ref, gate_ref, x_ref, gf_ref, lng_ref, lnb_ref, o_ref, *, alpha):
    gate = gate_ref[...]
    f = gate[:, 0:1] * y_ref[0]
    for k in range(1, TOP_K):
        f = f + gate[:, k:k + 1] * y_ref[k]
    y = alpha * x_ref[...] + gf_ref[...] * f
    o_ref[...] = _layer_norm_rows(y) * lng_ref[...] + lnb_ref[...]


def combine_block(y_slot, gate, x, mods, layer, rows_per_cond, cond_row0, ln_g, ln_b, alpha):
    rows, d = x.shape
    tm = 256
    row_of = lambda i: cond_row0 + (i * tm) // rows_per_cond
    vec = pl.BlockSpec((1, d), lambda i: (0, 0))
    return pl.pallas_call(
        functools.partial(_combine_kernel, alpha=alpha), grid=(rows // tm,),
        in_specs=[pl.BlockSpec((TOP_K, tm, d), lambda i: (0, i, 0)),
                  pl.BlockSpec((tm, LANE), lambda i: (i, 0)),
                  pl.BlockSpec((tm, d), lambda i: (i, 0)),
                  _mod_spec(layer, 5, row_of), vec, vec],
        out_specs=pl.BlockSpec((tm, d), lambda i: (i, 0)),
        out_shape=jax.ShapeDtypeStruct((rows, d), F32),
        compiler_params=_cparams(1), name="combine",
    )(y_slot.reshape(TOP_K, rows, d), gate, x, mods, ln_g, ln_b)


def _rope_tables(n_tokens):
    pos = jnp.arange(n_tokens)
    row = (pos // GRID_W).astype(F32)
    col = (pos % GRID_W).astype(F32)

    def angles(rot_dim):
        n_freq = rot_dim // 4
        inv = 1.0 / (ROPE_BASE ** (jnp.arange(n_freq, dtype=F32) / n_freq))
        ang = jnp.concatenate([row[:, None] * inv, col[:, None] * inv], -1)
        return jnp.cos(ang), jnp.sin(ang)

    c, s = angles(HEAD_DIM)
    rope_ab = (jnp.concatenate([c, c], -1), jnp.concatenate([-s, s], -1))
    c, s = angles(C_ROPE)
    one = jnp.ones((n_tokens, LANE - C_ROPE), F32)
    zero32 = jnp.zeros_like(s)
    zero64 = jnp.zeros((n_tokens, LANE - C_ROPE), F32)
    rope_c = (jnp.concatenate([c, c, one], -1), jnp.concatenate([-s, zero32, zero64], -1),
              jnp.concatenate([zero32, s, zero64], -1))
    return rope_ab, rope_c


def _group_layer(x, layer, lam_init, alpha, mods, cond_row0, n_batch, n_seq, wts, prm, caches, ropes):
    rows = n_batch * n_seq
    latent = caches is not None
    rope_ab, rope_c = ropes if latent else (None, None)

    rpc = n_seq if latent else rows
    z = ln_mod_matmul(x, mods, layer, 0, rpc, cond_row0, wts['w_main'], mode="rope" if latent else "plain",
                      rope=rope_ab, n_seq=n_seq)
    gates = ln_mod_matmul(x, mods, layer, 0, rpc, cond_row0, wts['w_gates'], mode="sigmoid", n_seq=n_seq)

    tq = min(512, n_seq)
    nqb = n_seq // tq
    zrow = lambda w, col: pl.BlockSpec((n_seq, w), lambda b, h, i: (b, col(h)))
    zq = lambda w, col: pl.BlockSpec((tq, w), lambda b, h, i: (b * nqb + i, col(h)))

    if latent:
        ck = caches['a_k'].reshape(caches['a_k'].shape[:3] + (A_KV_HEADS * HEAD_DIM,))
        cv = caches['a_v'].reshape(ck.shape)
        oa = banded_attention(z, ck, cv, layer, prm['sink'], n_batch, n_seq)
    else:
        oa = attention(
            (n_batch, A_HEADS, nqb),
            [(z, zq(HEAD_DIM, lambda h: COL_QA // HEAD_DIM + h))],
            [[(z, zrow(HEAD_DIM, lambda h: COL_KA // HEAD_DIM + h // A_GROUP))]],
            [(z, zrow(HEAD_DIM, lambda h: COL_VA // HEAD_DIM + h // A_GROUP))],
            zq(HEAD_DIM, lambda h: h), jax.ShapeDtypeStruct((rows, A_HEADS * HEAD_DIM), BF16),
            scale=HEAD_DIM ** -0.5, sink=prm['sink'], name="attn_a_ctx")

    qb0, kb0 = COL_QB // B_HEAD_DIM, COL_KB // B_HEAD_DIM
    k_ops = [[(z, zrow(B_HEAD_DIM, lambda h, m=m: kb0 + m * B_HEADS + h)) for m in range(2)]]
    v_ops = [(z, zrow(B_V_DIM, lambda h: COL_VB // B_V_DIM + h))]
    if latent:
        l_ctx = caches['b_k'].shape[2]
        cbk = caches['b_k'].reshape(caches['b_k'].shape[:3] + (2 * B_HEADS * B_HEAD_DIM,))
        cbv = caches['b_v'].reshape(caches['b_v'].shape[:3] + (B_HEADS * B_V_DIM,))
        cspec = lambda w, col: pl.BlockSpec((None, None, l_ctx, w), lambda b, h, i: (b, layer, 0, col(h)))
        k_ops = [[(cbk, cspec(B_HEAD_DIM, lambda h, m=m: m * B_HEADS + h)) for m in range(2)]] + k_ops
        v_ops = [(cbv, cspec(B_V_DIM, lambda h: h))] + v_ops
    vec3 = lambda a: (a.reshape(a.shape[0], 1, a.shape[1]),
                      pl.BlockSpec((None, 1, a.shape[1]), lambda b, h, i: (layer, 0, 0)))
    ob = attention(
        (n_batch, B_HEADS, nqb),
        [(z, zq(B_HEAD_DIM, lambda h, m=m: qb0 + m * B_HEADS + h)) for m in range(2)],
        k_ops, v_ops, zq(B_V_DIM, lambda h: h), jax.ShapeDtypeStruct((rows, B_HEADS * B_V_DIM), BF16),
        scale=B_HEAD_DIM ** -0.5,
        diff_ops=[vec3(prm['lam_q1']), vec3(prm['lam_k1']), vec3(prm['lam_q2']), vec3(prm['lam_k2']),
                  vec3(prm['subln'])],
        lam_init=lam_init, name="attn_b")

    q_c = c_query(z, prm['q_norm'], wts['w_uq'], rope_c, n_seq)
    tmc = min(512, n_seq)
    per_seq = n_seq // tmc
    kv_out = c_keyvalue(
        (z, pl.BlockSpec((tmc, C_KV_LORA), lambda i: (i, COL_CKV // C_KV_LORA))),
        (z, pl.BlockSpec((tmc, LANE), lambda i: (i, COL_KR // LANE))),
        rows, tmc, (rows // tmc,), prm['kv_norm'], wts['w_uk'], wts['w_uv'],
        rope_c=rope_c, rope_spec=pl.BlockSpec((tmc, LANE), lambda i: (i % per_seq, 0)), emit_norm=not latent)
    k_c, v_c = kv_out[0], kv_out[1]
    crow = lambda w: pl.BlockSpec((n_seq, w), lambda b, h, i: (b, h))
    cq_spec = lambda w: pl.BlockSpec((tq, w), lambda b, h, i: (b * nqb + i, h))
    k_ops = [[(k_c, crow(C_QK_PAD))]]
    v_ops = [(v_c, crow(C_VDIM))]
    if latent:
        l_ctx = caches['c_kv'].shape[2]
        kx, vx = c_keyvalue(
            (caches['c_kv'], pl.BlockSpec((None, None, l_ctx, C_KV_LORA), lambda b, s: (b, layer, 0, 0))),
            (caches['c_kr'], pl.BlockSpec((None, None, l_ctx, LANE), lambda b, s: (b, layer, 0, 0))),
            n_batch * l_ctx, l_ctx, (n_batch, 1), None, wts['w_uk'], wts['w_uv'])
        xrow = lambda w: pl.BlockSpec((l_ctx, w), lambda b, h, i: (b, h))
        k_ops = [[(kx, xrow(C_QK_PAD))]] + k_ops
        v_ops = [(vx, xrow(C_VDIM))] + v_ops
    oc = attention(
        (n_batch, C_HEADS, nqb), [(q_c, cq_spec(C_QK_PAD))], k_ops, v_ops,
        cq_spec(C_VDIM), jax.ShapeDtypeStruct((rows, C_HEADS * C_VDIM), BF16),
        scale=(C_NOPE + C_ROPE) ** -0.5, name="attn_c")

    x = merge_block(oa, ob, oc, gates, x, mods, layer, rpc, cond_row0, prm['ln1_g'], prm['ln1_b'],
                    wts['w_br_a'], wts['w_br_b'], wts['w_br_c'], wts['w_out'], alpha)

    h, route, gate, cnt = router(x, mods, layer, rpc, cond_row0, wts['w_router'], prm['b_router'])
    meta = routing_metadata(route, cnt, rows)
    y_slot = expert_ffn(h, meta, layer, wts['w_gu'], wts['b_gu'], wts['w_down'], wts['b_down'])
    x = combine_block(y_slot, gate, x, mods, layer, rpc, cond_row0, prm['ln2_g'], prm['ln2_b'], alpha)

    state = None
    if not latent:
        state = (z[:, COL_KA:COL_KA + 256], z[:, COL_VA:COL_VA + 256], z[:, COL_KB:COL_KB + 1024],
                 z[:, COL_VB:COL_VB + 1024], kv_out[2], z[:, COL_KR:COL_KR + C_ROPE])
    return x, state


def kernel(x_prompt, x_sample, c, cache_a_k, cache_a_v, cache_b_k, cache_b_v, cache_c_kv, cache_c_kr, c_ctx, w_ada, b_ada, w_in, sink_a, lam_q1, lam_k1, lam_q2, lam_k2, subln_b, q_norm_c, kv_norm_c, w_uq, w_ukv, w_br_a, w_br_b, w_br_c, w_out, ln1_g, ln1_b, ln2_g, ln2_b, w_router, b_router, w_gu, b_gu, w_down, b_down):
    depth = w_in.shape[0]
    batch, seq, d = x_prompt.shape
    dec_batch, dec_seq, _ = x_sample.shape
    alpha = (2 * depth) ** 0.25
    assert d == D_MODEL and dec_batch + 1 <= 8

    cond8 = jnp.zeros((8, d), F32).at[0].set(c_ctx).at[1:1 + dec_batch].set(c)
    mods = adaln_all(cond8, w_ada, b_ada).reshape(depth, 8, 6, 1, d)

    ropes = _rope_tables(dec_seq)
    n_exp = w_gu.shape[1]
    w_gu_b = w_gu.astype(BF16)
    w_down_b = w_down.astype(BF16)
    b_gu4 = b_gu.reshape(depth, n_exp, 1, 2 * D_FF)
    b_down4 = b_down.reshape(depth, n_exp, 1, d)
    caches = dict(a_k=cache_a_k, a_v=cache_a_v, b_k=cache_b_k, b_v=cache_b_v, c_kv=cache_c_kv,
                  c_kr=jnp.pad(cache_c_kr, ((0, 0), (0, 0), (0, 0), (0, LANE - C_ROPE))))

    xp = x_prompt.reshape(batch * seq, d)
    xs = x_sample.reshape(dec_batch * dec_seq, d)
    states = []
    for l in range(depth):
        lam_init = 0.8 - 0.6 * math.exp(-0.3 * l)
        wl = w_in[l]
        w_main = jnp.concatenate([wl[:, :N_IN_SRC], jnp.zeros((d, N_MAIN - N_IN_SRC), F32)], axis=1).astype(BF16)
        w_uq_l = w_uq[l].reshape(C_Q_LORA, C_HEADS, C_NOPE + C_ROPE)
        w_uq_p = jnp.pad(w_uq_l, ((0, 0), (0, 0), (0, C_QK_PAD - C_NOPE - C_ROPE))).reshape(C_Q_LORA, -1).astype(BF16)
        w_ukv_l = w_ukv[l].reshape(C_KV_LORA, C_HEADS, C_NOPE + C_VDIM)
        wts = dict(
            w_main=w_main, w_gates=wl[:, N_IN_SRC:].astype(BF16), w_uq=w_uq_p,
            w_uk=w_ukv_l[:, :, :C_NOPE].reshape(C_KV_LORA, -1).astype(BF16),
            w_uv=w_ukv_l[:, :, C_NOPE:].reshape(C_KV_LORA, -1).astype(BF16),
            w_br_a=w_br_a[l].astype(BF16), w_br_b=w_br_b[l].astype(BF16), w_br_c=w_br_c[l].astype(BF16),
            w_out=w_out[l].astype(BF16),
            w_router=jnp.pad(w_router[l], ((0, 0), (0, LANE - n_exp))),
            w_gu=w_gu_b, b_gu=b_gu4, w_down=w_down_b, b_down=b_down4)
        prm = dict(sink=sink_a[l], lam_q1=lam_q1, lam_k1=lam_k1, lam_q2=lam_q2, lam_k2=lam_k2, subln=subln_b,
                   q_norm=q_norm_c[l][None], kv_norm=kv_norm_c[l][None],
                   ln1_g=ln1_g[l][None], ln1_b=ln1_b[l][None], ln2_g=ln2_g[l][None], ln2_b=ln2_b[l][None],
                   b_router=jnp.pad(b_router[l], (0, LANE - n_exp))[None])
        xp, st = _group_layer(xp, l, lam_init, alpha, mods, 0, batch, seq, wts, prm, None, None)
        xs, _ = _group_layer(xs, l, lam_init, alpha, mods, 1, dec_batch, dec_seq, wts, prm, caches, ropes)
        states.append(st)

    def stack(i, shape):
        return jnp.stack([s[i].reshape((batch, seq) + shape) for s in states], axis=1)

    return (xp.reshape(batch, seq, d), xs.reshape(dec_batch, dec_seq, d),
            stack(0, (A_KV_HEADS, HEAD_DIM)), stack(1, (A_KV_HEADS, HEAD_DIM)),
            stack(2, (2, B_HEADS, B_HEAD_DIM)), stack(3, (B_HEADS, B_V_DIM)),
            stack(4, (C_KV_LORA,)), stack(5, (C_ROPE,)))
```

```python
import functools
import math

import jax
import jax.numpy as jnp
from jax import lax
from jax.experimental import pallas as pl
from jax.experimental.pallas import tpu as pltpu

F32 = jnp.float32
BF16 = jnp.bfloat16

D_MODEL = 2048
GRID_W = 64
WINDOW = 128
A_HEADS = 8
A_KV_HEADS = 2
A_GROUP = A_HEADS // A_KV_HEADS
HEAD_DIM = 128
B_HEADS = 4
B_HEAD_DIM = 128
B_V_DIM = 2 * B_HEAD_DIM
C_HEADS = 8
C_NOPE = 128
C_ROPE = 64
C_VDIM = 128
C_Q_LORA = 512
C_KV_LORA = 256
C_QK_PAD = 256
N_BRANCH = 3
N_EXPERTS = 32
TOP_K = 4
D_FF = 2048
SWIGLU_LIMIT = 7.0
SWIGLU_ALPHA = 1.702
ROPE_BASE = 10000.0
LN_EPS = 1e-5
RMS_EPS = 1e-6
NEG_INF = -1e30

COL_QA = 0
COL_KA = 1024
COL_VA = 1280
COL_QB = 1536
COL_KB = 2560
COL_VB = 3584
COL_CQ = 4608
COL_CKV = 5120
COL_KR = 5376
N_MAIN = 5632
N_GATES = N_BRANCH * D_MODEL
N_IN_SRC = COL_KR + C_ROPE

LANE = 128
V7X_VMEM_LIMIT = 56 * 1024 * 1024

LOG2_E = 1.4426950408889634

PROJ_TM = 1024
PROJ_TN = 512
ATTN_SUB = 256
MERGE_TM = 256
ROUTER_TM = 256
MOE_SB = 256
MOE_MAX_SB = 6
MOE_FF = 256
DISPATCH_TM = 256
COMBINE_TM = 128


def _cparams(n_grid):
    return pltpu.CompilerParams(dimension_semantics=("arbitrary",) * n_grid, vmem_limit_bytes=V7X_VMEM_LIMIT)


def _sigmoid(x):
    return 1.0 / (1.0 + jnp.exp(-x))


def _layer_norm_rows(x):
    mu = jnp.mean(x, axis=-1, keepdims=True)
    xc = x - mu
    var = jnp.mean(xc * xc, axis=-1, keepdims=True)
    return xc * lax.rsqrt(var + LN_EPS)


def _rms_rows(x, g):
    return x * lax.rsqrt(jnp.mean(x * x, axis=-1, keepdims=True) + RMS_EPS) * g


def _rope_half128(x, cosf, sinf):
    return x * cosf + pltpu.roll(x, 64, 1) * sinf


def _rope_pad64(x, cf, sa, sb):
    return x * cf + pltpu.roll(x, 96, 1) * sa + pltpu.roll(x, 32, 1) * sb


def _adaln_kernel(c_ref, w_ref, b_ref, o_ref):
    c = c_ref[...]
    s = (c * _sigmoid(c)).astype(BF16)
    o_ref[...] = jnp.dot(s, w_ref[...].astype(BF16), preferred_element_type=F32) + b_ref[...]


def adaln_all(cond8, w_ada, b_ada):
    depth, d, n = w_ada.shape
    tn = 1024
    return pl.pallas_call(
        _adaln_kernel,
        grid=(depth, n // tn),
        in_specs=[
            pl.BlockSpec((8, d), lambda l, j: (0, 0)),
            pl.BlockSpec((None, d, tn), lambda l, j: (l, 0, j)),
            pl.BlockSpec((None, 1, tn), lambda l, j: (l, 0, j)),
        ],
        out_specs=pl.BlockSpec((None, 8, tn), lambda l, j: (l, 0, j)),
        out_shape=jax.ShapeDtypeStruct((depth, 8, n), F32),
        compiler_params=_cparams(2),
        name="adaln",
    )(cond8, w_ada, b_ada.reshape(depth, 1, n))


def _mod_spec(layer, slot, row_of_tile):
    return pl.BlockSpec((None, None, None, 1, D_MODEL), lambda *g: (layer, row_of_tile(*g), slot, 0, 0))


ROPE_FULL_TILES = (0, 1, 3, 4, 5, 6)
ROPE_HALF_TILE = 2


def _lnmm_kernel(*refs, mode):
    if mode == "rope":
        x_ref, sc_ref, sh_ref, w_ref, cos_ref, sin_ref, o_ref, h_scr = refs
    else:
        x_ref, sc_ref, sh_ref, w_ref, o_ref, h_scr = refs
    j = pl.program_id(1)

    @pl.when(j == 0)
    def _():
        h = _layer_norm_rows(x_ref[...]) * (1.0 + sc_ref[...]) + sh_ref[...]
        h_scr[...] = h.astype(BF16)

    acc = jnp.dot(h_scr[...], w_ref[...], preferred_element_type=F32)
    if mode == "sigmoid":
        o_ref[...] = _sigmoid(acc)
    elif mode == "plain":
        o_ref[...] = acc
    else:
        is_full = functools.reduce(jnp.logical_or, [j == t for t in ROPE_FULL_TILES])
        is_half = j == ROPE_HALF_TILE

        def rotated(n_heads):
            cosf, sinf = cos_ref[...], sin_ref[...]
            for hh in range(PROJ_TN // HEAD_DIM):
                blk = acc[:, hh * HEAD_DIM:(hh + 1) * HEAD_DIM]
                if hh < n_heads:
                    blk = _rope_half128(blk, cosf, sinf)
                o_ref[:, hh * HEAD_DIM:(hh + 1) * HEAD_DIM] = blk

        @pl.when(is_full)
        def _():
            rotated(PROJ_TN // HEAD_DIM)

        @pl.when(is_half)
        def _():
            rotated(A_KV_HEADS)

        @pl.when(jnp.logical_not(jnp.logical_or(is_full, is_half)))
        def _():
            o_ref[...] = acc


def ln_mod_matmul(x, mods, layer, slot, rows_per_cond, cond_row0, w, *, mode, rope=None, n_seq=None):
    rows, d = x.shape
    n = w.shape[1]
    tm = min(PROJ_TM, n_seq if mode == "rope" else rows_per_cond)
    tn = PROJ_TN
    row_of = lambda i, j: cond_row0 + (i * tm) // rows_per_cond
    in_specs = [
        pl.BlockSpec((tm, d), lambda i, j: (i, 0)),
        _mod_spec(layer, slot + 1, row_of),
        _mod_spec(layer, slot, row_of),
        pl.BlockSpec((d, tn), lambda i, j: (0, j)),
    ]
    args = [x, mods, mods, w]
    if mode == "rope":
        per_seq = n_seq // tm
        in_specs += [pl.BlockSpec((tm, LANE), lambda i, j: (i % per_seq, 0))] * 2
        args += list(rope)
    return pl.pallas_call(
        functools.partial(_lnmm_kernel, mode=mode),
        grid=(rows // tm, n // tn),
        in_specs=in_specs,
        out_specs=pl.BlockSpec((tm, tn), lambda i, j: (i, j)),
        out_shape=jax.ShapeDtypeStruct((rows, n), F32),
        scratch_shapes=[pltpu.VMEM((tm, d), BF16)],
        compiler_params=_cparams(2),
        name="ln_mod_matmul_" + mode,
    )(*args)


def _attn_kernel(*refs, n_maps, n_seg, scale, use_sink, diff, lam_init):
    pos = 0
    if use_sink:
        sink_ref = refs[0]
        pos = 1
    q_refs = refs[pos:pos + n_maps]
    pos += n_maps
    k_refs = [refs[pos + sg * n_maps: pos + (sg + 1) * n_maps] for sg in range(n_seg)]
    pos += n_seg * n_maps
    v_refs = refs[pos:pos + n_seg]
    pos += n_seg
    if diff:
        lq1, lk1, lq2, lk2, subln_ref = refs[pos:pos + 5]
        pos += 5
    o_ref = refs[pos]

    c2 = scale * LOG2_E
    ks = [[k_refs[sg][m][...].astype(BF16) for m in range(n_maps)] for sg in range(n_seg)]
    vs = [v_refs[sg][...].astype(BF16) for sg in range(n_seg)]
    if use_sink:
        sink_raw = sink_ref[pl.program_id(1)] * (1.0 / scale)
    if diff:
        f = jnp.sum(lq1[...] * lk1[...], axis=-1, keepdims=True)
        g = jnp.sum(lq2[...] * lk2[...], axis=-1, keepdims=True)
        lam = jnp.exp(f) - jnp.exp(g) + lam_init
    tq = o_ref.shape[0]
    sub = min(tq, ATTN_SUB)
    for r0 in range(0, tq, sub):
        probs = []
        for m in range(n_maps):
            q = q_refs[m][r0:r0 + sub, :].astype(BF16)
            s = [lax.dot_general(q, ks[sg][m], (((1,), (1,)), ((), ())), preferred_element_type=F32)
                 for sg in range(n_seg)]
            mx = functools.reduce(jnp.maximum, [jnp.max(t, axis=-1, keepdims=True) for t in s])
            if use_sink:
                mx = jnp.maximum(mx, sink_raw)
            e = [jnp.exp2((t - mx) * c2) for t in s]
            den = functools.reduce(jnp.add, [jnp.sum(t, axis=-1, keepdims=True) for t in e])
            if use_sink:
                den = den + jnp.exp2((sink_raw - mx) * c2)
            inv = 1.0 / den
            probs.append([t * inv for t in e])
        if diff:
            a = [probs[0][sg] - lam * probs[1][sg] for sg in range(n_seg)]
        else:
            a = probs[0]
        o = functools.reduce(jnp.add, [jnp.dot(a[sg].astype(BF16), vs[sg], preferred_element_type=F32)
                                       for sg in range(n_seg)])
        if diff:
            o = _rms_rows(o, subln_ref[...]) * (1.0 - lam_init)
        o_ref[r0:r0 + sub, :] = o.astype(o_ref.dtype)


def attention(grid, q_ops, k_ops, v_ops, out_spec, out_shape, *, scale, sink=None, diff_ops=None, lam_init=0.0,
              name="attn"):
    n_maps, n_seg = len(q_ops), len(k_ops)
    ops = []
    if sink is not None:
        ops.append((sink, pl.BlockSpec(memory_space=pltpu.SMEM)))
    ops += list(q_ops)
    for seg in k_ops:
        ops += list(seg)
    ops += list(v_ops)
    if diff_ops is not None:
        ops += list(diff_ops)
    kern = functools.partial(_attn_kernel, n_maps=n_maps, n_seg=n_seg, scale=scale, use_sink=sink is not None,
                             diff=diff_ops is not None, lam_init=lam_init)
    return pl.pallas_call(
        kern, grid=grid,
        in_specs=[s for _, s in ops],
        out_specs=out_spec, out_shape=out_shape,
        compiler_params=_cparams(len(grid)), name=name,
    )(*[a for a, _ in ops])


def _band_kernel(sink_ref, q_ref, kp_ref, kc_ref, kn_ref, kx_ref, vp_ref, vc_ref, vn_ref, vx_ref, o_ref, *, n_blocks):
    g = pl.program_id(1)
    i = pl.program_id(2)
    w = WINDOW
    q = q_ref[...]
    qs = jnp.concatenate([q[:, r * HEAD_DIM:(r + 1) * HEAD_DIM] for r in range(A_GROUP)], axis=0).astype(BF16)
    kcat = jnp.concatenate([kp_ref[...], kc_ref[...], kn_ref[...], kx_ref[...]], axis=0).astype(BF16)
    vcat = jnp.concatenate([vp_ref[...], vc_ref[...], vn_ref[...], vx_ref[...]], axis=0).astype(BF16)
    s = lax.dot_general(qs, kcat, (((1,), (1,)), ((), ())), preferred_element_type=F32) * (HEAD_DIM ** -0.5)
    n_keys = s.shape[1]
    row = lax.broadcasted_iota(jnp.int32, s.shape, 0) & (w - 1)
    col = lax.broadcasted_iota(jnp.int32, s.shape, 1)
    prev_slack = jnp.where(i > 0, col - row, -1)
    next_slack = jnp.where(i < n_blocks - 1, row - col + 2 * w, -1)
    slack = jnp.where(col < w, prev_slack, jnp.where(col < 2 * w, 0, jnp.where(col < 3 * w, next_slack, 0)))
    s = jnp.where(slack >= 0, s, NEG_INF)
    sink = jnp.concatenate([jnp.full((w, 1), sink_ref[g * A_GROUP + r], F32) for r in range(A_GROUP)], axis=0)
    mx = jnp.maximum(jnp.max(s, axis=-1, keepdims=True), sink)
    e = jnp.exp(s - mx)
    den = jnp.sum(e, axis=-1, keepdims=True) + jnp.exp(sink - mx)
    p = (e * (1.0 / den)).astype(BF16)
    o = jnp.dot(p, vcat, preferred_element_type=F32)
    for r in range(A_GROUP):
        o_ref[:, r * HEAD_DIM:(r + 1) * HEAD_DIM] = o[r * w:(r + 1) * w].astype(o_ref.dtype)


def banded_attention(z, cache_k, cache_v, layer, sink, n_batch, n_seq):
    nb = n_seq // WINDOW
    l_ctx = cache_k.shape[2]
    qw = A_GROUP * HEAD_DIM
    kcol, vcol = COL_KA // HEAD_DIM, COL_VA // HEAD_DIM

    def blk(col0, shift):
        def imap(b, g, i):
            return (b * nb + jnp.clip(i + shift, 0, nb - 1), col0 + g)
        return pl.BlockSpec((WINDOW, HEAD_DIM), imap)

    ctx_spec = pl.BlockSpec((None, None, l_ctx, HEAD_DIM), lambda b, g, i: (b, layer, 0, g))
    return pl.pallas_call(
        functools.partial(_band_kernel, n_blocks=nb),
        grid=(n_batch, A_KV_HEADS, nb),
        in_specs=[pl.BlockSpec(memory_space=pltpu.SMEM),
                  pl.BlockSpec((WINDOW, qw), lambda b, g, i: (b * nb + i, g)),
                  blk(kcol, -1), blk(kcol, 0), blk(kcol, 1), ctx_spec,
                  blk(vcol, -1), blk(vcol, 0), blk(vcol, 1), ctx_spec],
        out_specs=pl.BlockSpec((WINDOW, qw), lambda b, g, i: (b * nb + i, g)),
        out_shape=jax.ShapeDtypeStruct((n_batch * n_seq, A_HEADS * HEAD_DIM), BF16),
        compiler_params=_cparams(3), name="banded_attn",
    )(sink, z, z, z, z, cache_k, z, z, z, cache_v)


def _cq_kernel(*refs, rope):
    if rope:
        cq_ref, g_ref, w_ref, cf_ref, sa_ref, sb_ref, o_ref = refs
    else:
        cq_ref, g_ref, w_ref, o_ref = refs
    y = _rms_rows(cq_ref[...], g_ref[...]).astype(BF16)
    q = jnp.dot(y, w_ref[...], preferred_element_type=F32)
    if rope:
        cf, sa, sb = cf_ref[...], sa_ref[...], sb_ref[...]
    for h in range(C_HEADS):
        lo = h * C_QK_PAD
        o_ref[:, lo:lo + C_NOPE] = q[:, lo:lo + C_NOPE].astype(o_ref.dtype)
        blk = q[:, lo + C_NOPE:lo + C_QK_PAD]
        if rope:
            blk = _rope_pad64(blk, cf, sa, sb)
        o_ref[:, lo + C_NOPE:lo + C_QK_PAD] = blk.astype(o_ref.dtype)


def c_query(z, q_norm, w_uq_p, rope_c, n_seq):
    rows = z.shape[0]
    rope = rope_c is not None
    tm = min(512, n_seq if rope else rows)
    in_specs = [pl.BlockSpec((tm, C_Q_LORA), lambda i: (i, COL_CQ // C_Q_LORA)),
                pl.BlockSpec((1, C_Q_LORA), lambda i: (0, 0)),
                pl.BlockSpec(w_uq_p.shape, lambda i: (0, 0))]
    args = [z, q_norm, w_uq_p]
    if rope:
        per_seq = n_seq // tm
        in_specs += [pl.BlockSpec((tm, LANE), lambda i: (i % per_seq, 0))] * 3
        args += list(rope_c)
    return pl.pallas_call(
        functools.partial(_cq_kernel, rope=rope),
        grid=(rows // tm,), in_specs=in_specs,
        out_specs=pl.BlockSpec((tm, C_HEADS * C_QK_PAD), lambda i: (i, 0)),
        out_shape=jax.ShapeDtypeStruct((rows, C_HEADS * C_QK_PAD), BF16),
        compiler_params=_cparams(1), name="c_query",
    )(*args)


def _ckv_kernel(*refs, normalize, rope, emit_norm):
    refs = list(refs)
    ckv_ref, kr_ref = refs[:2]
    pos = 2
    if normalize:
        g_ref = refs[pos]
        pos += 1
    wn_ref, wv_ref = refs[pos:pos + 2]
    pos += 2
    if rope:
        cf_ref, sa_ref, sb_ref = refs[pos:pos + 3]
        pos += 3
    k_ref, v_ref = refs[pos:pos + 2]
    pos += 2
    ckv = ckv_ref[...]
    if normalize:
        ckv = _rms_rows(ckv, g_ref[...])
    if emit_norm:
        refs[pos][...] = ckv
    cb = ckv.astype(BF16)
    kn = jnp.dot(cb, wn_ref[...], preferred_element_type=F32)
    v_ref[...] = jnp.dot(cb, wv_ref[...], preferred_element_type=F32).astype(v_ref.dtype)
    kr = kr_ref[...]
    if rope:
        kr = _rope_pad64(kr, cf_ref[...], sa_ref[...], sb_ref[...])
    kr = kr.astype(k_ref.dtype)
    for h in range(C_HEADS):
        lo = h * C_QK_PAD
        k_ref[:, lo:lo + C_NOPE] = kn[:, h * C_NOPE:(h + 1) * C_NOPE].astype(k_ref.dtype)
        k_ref[:, lo + C_NOPE:lo + C_QK_PAD] = kr


def c_keyvalue(ckv_op, kr_op, rows, tm, grid, kv_norm, w_nope, w_v, rope_c=None, rope_spec=None, emit_norm=False):
    ops = [ckv_op, kr_op]
    full = lambda a: (a, pl.BlockSpec(a.shape, lambda *g: (0,) * a.ndim))
    if kv_norm is not None:
        ops.append(full(kv_norm))
    ops += [full(w_nope), full(w_v)]
    if rope_c is not None:
        ops += [(t, rope_spec) for t in rope_c]
    row_map = lambda *g: (functools.reduce(lambda a, b: a * grid[1] + b, g) if len(g) > 1 else g[0], 0)
    out_specs = [pl.BlockSpec((tm, C_HEADS * C_QK_PAD), row_map), pl.BlockSpec((tm, C_HEADS * C_VDIM), row_map)]
    out_shape = [jax.ShapeDtypeStruct((rows, C_HEADS * C_QK_PAD), BF16),
                 jax.ShapeDtypeStruct((rows, C_HEADS * C_VDIM), BF16)]
    if emit_norm:
        out_specs.append(pl.BlockSpec((tm, C_KV_LORA), row_map))
        out_shape.append(jax.ShapeDtypeStruct((rows, C_KV_LORA), F32))
    return pl.pallas_call(
        functools.partial(_ckv_kernel, normalize=kv_norm is not None, rope=rope_c is not None, emit_norm=emit_norm),
        grid=grid, in_specs=[s for _, s in ops], out_specs=out_specs, out_shape=out_shape,
        compiler_params=_cparams(len(grid)), name="c_keyvalue",
    )(*[a for a, _ in ops])


def _merge_kernel(oa_ref, ob_ref, oc_ref, g_ref, x_ref, ga_ref, lng_ref, lnb_ref, wa_ref, wb_ref, wc_ref, wo_ref,
                  o_ref, mix_scr, *, alpha):
    d = D_MODEL
    chunk = 512
    for c in range(d // chunk):
        sl = slice(c * chunk, (c + 1) * chunk)
        ya = jnp.dot(oa_ref[...], wa_ref[:, sl], preferred_element_type=F32)
        yb = jnp.dot(ob_ref[...], wb_ref[:, sl], preferred_element_type=F32)
        yc = jnp.dot(oc_ref[...], wc_ref[:, sl], preferred_element_type=F32)
        mixed = (g_ref[:, c * chunk:(c + 1) * chunk] * ya
                 + g_ref[:, d + c * chunk:d + (c + 1) * chunk] * yb
                 + g_ref[:, 2 * d + c * chunk:2 * d + (c + 1) * chunk] * yc)
        mix_scr[:, sl] = mixed.astype(BF16)
    out = jnp.dot(mix_scr[...], wo_ref[...], preferred_element_type=F32)
    y = alpha * x_ref[...] + ga_ref[...] * out
    o_ref[...] = _layer_norm_rows(y) * lng_ref[...] + lnb_ref[...]


def merge_block(oa, ob, oc, gates, x, mods, layer, rows_per_cond, cond_row0, ln_g, ln_b, wa, wb, wc, wo, alpha):
    rows, d = x.shape
    tm = MERGE_TM
    row_of = lambda i: cond_row0 + (i * tm) // rows_per_cond
    const = lambda a: pl.BlockSpec(a.shape, lambda i: (0, 0), pipeline_mode=pl.Buffered(1))
    rowblk = lambda a: pl.BlockSpec((tm, a.shape[1]), lambda i: (i, 0))
    vec = pl.BlockSpec((1, d), lambda i: (0, 0))
    return pl.pallas_call(
        functools.partial(_merge_kernel, alpha=alpha),
        grid=(rows // tm,),
        in_specs=[rowblk(oa), rowblk(ob), rowblk(oc), rowblk(gates), rowblk(x), _mod_spec(layer, 2, row_of),
                  vec, vec, const(wa), const(wb), const(wc), const(wo)],
        out_specs=pl.BlockSpec((tm, d), lambda i: (i, 0)),
        out_shape=jax.ShapeDtypeStruct((rows, d), F32),
        scratch_shapes=[pltpu.VMEM((tm, d), BF16)],
        compiler_params=_cparams(1), name="merge",
    )(oa, ob, oc, gates, x, mods, ln_g, ln_b, wa, wb, wc, wo)


ROUTE_RANK_LANE = TOP_K


def _router_kernel(x_ref, sc_ref, sh_ref, wr_ref, br_ref, h_ref, route_ref, gate_ref, cnt_ref, carry_scr):
    i = pl.program_id(0)

    @pl.when(i == 0)
    def _():
        carry_scr[...] = jnp.zeros_like(carry_scr)

    h = _layer_norm_rows(x_ref[...]) * (1.0 + sc_ref[...]) + sh_ref[...]
    h_ref[...] = h
    logits = jnp.dot(h, wr_ref[...], preferred_element_type=F32, precision=lax.Precision.HIGHEST) + br_ref[...]
    tm = logits.shape[0]
    lane = lax.broadcasted_iota(jnp.int32, (tm, LANE), 1)
    cur = jnp.where(lane < N_EXPERTS, logits, -jnp.inf)
    vals, idxs = [], []
    for _ in range(TOP_K):
        m = jnp.max(cur, axis=-1, keepdims=True)
        ik = jnp.min(jnp.where(cur == m, lane, LANE), axis=-1, keepdims=True)
        vals.append(m)
        idxs.append(ik)
        cur = jnp.where(lane == ik, -jnp.inf, cur)
    e = [jnp.exp(v - vals[0]) for v in vals]
    inv = 1.0 / functools.reduce(jnp.add, e)
    onehot = functools.reduce(jnp.add, [(lane == ik).astype(F32) for ik in idxs])
    r_i = lax.broadcasted_iota(jnp.int32, (tm, tm), 0)
    c_i = lax.broadcasted_iota(jnp.int32, (tm, tm), 1)
    tri = (c_i < r_i).astype(BF16)
    rank_full = jnp.dot(tri, onehot.astype(BF16), preferred_element_type=F32) + carry_scr[...]
    route = jnp.zeros((tm, LANE), jnp.int32)
    gate = jnp.zeros((tm, LANE), F32)
    for k in range(TOP_K):
        rk = jnp.sum(jnp.where(lane == idxs[k], rank_full, 0.0), axis=-1, keepdims=True).astype(jnp.int32)
        route = jnp.where(lane == k, idxs[k], route)
        route = jnp.where(lane == ROUTE_RANK_LANE + k, rk, route)
        gate = jnp.where(lane == k, e[k] * inv, gate)
    route_ref[...] = route
    gate_ref[...] = gate
    carry_scr[...] = carry_scr[...] + jnp.sum(onehot, axis=0, keepdims=True)
    cnt_ref[...] = carry_scr[...]


def router(x, mods, layer, rows_per_cond, cond_row0, w_router_p, b_router_p):
    rows, d = x.shape
    tm = ROUTER_TM
    row_of = lambda i: cond_row0 + (i * tm) // rows_per_cond
    rowblk = lambda w: pl.BlockSpec((tm, w), lambda i: (i, 0))
    return pl.pallas_call(
        _router_kernel, grid=(rows // tm,),
        in_specs=[rowblk(d), _mod_spec(layer, 4, row_of), _mod_spec(layer, 3, row_of),
                  pl.BlockSpec((d, LANE), lambda i: (0, 0)), pl.BlockSpec((1, LANE), lambda i: (0, 0))],
        out_specs=[rowblk(d), rowblk(LANE), rowblk(LANE), pl.BlockSpec((1, LANE), lambda i: (0, 0))],
        out_shape=[jax.ShapeDtypeStruct((rows, d), F32), jax.ShapeDtypeStruct((rows, LANE), jnp.int32),
                   jax.ShapeDtypeStruct((rows, LANE), F32), jax.ShapeDtypeStruct((1, LANE), F32)],
        scratch_shapes=[pltpu.VMEM((1, LANE), F32)],
        compiler_params=_cparams(1), name="router",
    )(x, mods, mods, w_router_p, b_router_p)


def _dispatch_kernel(dest, h_ref, init_hbm, x_hbm, sem):
    del init_hbm
    i = pl.program_id(0)
    tm = h_ref.shape[0]
    for t in range(tm):
        for k in range(TOP_K):
            row = dest[(i * tm + t) * TOP_K + k]
            pltpu.make_async_copy(h_ref.at[pl.ds(t, 1)], x_hbm.at[pl.ds(row, 1)], sem).start()
    for k in range(TOP_K):
        pltpu.make_async_copy(h_ref, x_hbm.at[pl.ds(0, tm)], sem).wait()


def dispatch_rows(h, dest, cap):
    t, d = h.shape
    tm = DISPATCH_TM
    grid_spec = pltpu.PrefetchScalarGridSpec(
        num_scalar_prefetch=1, grid=(t // tm,),
        in_specs=[pl.BlockSpec((tm, d), lambda i, dst: (i, 0)), pl.BlockSpec(memory_space=pl.ANY)],
        out_specs=pl.BlockSpec(memory_space=pl.ANY),
        scratch_shapes=[pltpu.SemaphoreType.DMA(())])
    return pl.pallas_call(
        _dispatch_kernel, grid_spec=grid_spec,
        out_shape=jax.ShapeDtypeStruct((cap, d), F32),
        input_output_aliases={2: 0},
        compiler_params=_cparams(1), name="dispatch",
    )(dest, h, jnp.zeros((cap, d), F32))


def _expert_kernel(unit_e, unit_row0, unit_nsb, unit_act,
                   x_hbm, wg_ref, wu_ref, bg_ref, bu_ref, wd_ref, bd_ref, y_hbm,
                   xs, stage, acc, wgb, wub, wdb, sem_in, sem_out, *, n_ff):
    del unit_e
    u = pl.program_id(0)
    j = pl.program_id(1)
    sb = MOE_SB
    nsb = unit_nsb[u]
    row0 = unit_row0[u]
    active = unit_act[u] == 1

    def rows(rb, base=0):
        return pl.ds(pl.multiple_of(base + rb * sb, sb), sb)

    def in_copy(rb, slot):
        return pltpu.make_async_copy(x_hbm.at[rows(rb, row0)], stage.at[slot], sem_in.at[slot])

    def out_copy(rb):
        return pltpu.make_async_copy(acc.at[rows(rb)], y_hbm.at[rows(rb, row0)], sem_out)

    @pl.when((u == 0) & (j == 0))
    def _():
        acc[...] = jnp.zeros_like(acc)

    @pl.when(active & (j == 0))
    def _():
        in_copy(0, 0).start()

        def load(rb, carry):
            slot = rb % 2

            @pl.when(rb + 1 < nsb)
            def _():
                in_copy(rb + 1, 1 - slot).start()

            in_copy(rb, slot).wait()
            xs[rows(rb), :] = stage[slot].astype(BF16)
            return carry

        lax.fori_loop(0, nsb, load, 0)

    @pl.when(active)
    def _():
        wgb[...] = wg_ref[...].astype(BF16)
        wub[...] = wu_ref[...].astype(BF16)
        wdb[...] = wd_ref[...].astype(BF16)
        first = j == 0

        def ffn(rb, carry):
            xb = xs[rows(rb), :]
            g = jnp.dot(xb, wgb[...], preferred_element_type=F32) + bg_ref[...]
            v = jnp.dot(xb, wub[...], preferred_element_type=F32) + bu_ref[...]
            g = jnp.minimum(g, SWIGLU_LIMIT)
            v = jnp.clip(v, -SWIGLU_LIMIT, SWIGLU_LIMIT)
            a = (g * _sigmoid(SWIGLU_ALPHA * g) * (v + 1.0)).astype(BF16)
            contrib = jnp.dot(a, wdb[...], preferred_element_type=F32)
            base = jnp.where(first, jnp.broadcast_to(bd_ref[...], contrib.shape), acc[rows(rb), :])
            acc[rows(rb), :] = base + contrib
            return carry

        lax.fori_loop(0, nsb, ffn, 0)

    @pl.when(active & (j == n_ff - 1))
    def _():
        lax.fori_loop(0, nsb, lambda rb, c: (out_copy(rb).start(), c)[1], 0)
        lax.fori_loop(0, nsb, lambda rb, c: (out_copy(rb).wait(), c)[1], 0)


def expert_ffn(x_buf, units, layer, w_gu, b_gu, w_down, b_down):
    cap, d = x_buf.shape
    unit_e, unit_row0, unit_nsb, unit_act = units
    n_units = unit_e.shape[0]
    ff = MOE_FF
    n_ff = D_FF // ff
    max_r = MOE_SB * MOE_MAX_SB

    def wspec(shape, imap):
        def index_map(u, j, ue, r0, nsb, act):
            return imap(ue[u], jnp.where(act[u] == 1, j, n_ff - 1))
        return pl.BlockSpec(shape, index_map)

    grid_spec = pltpu.PrefetchScalarGridSpec(
        num_scalar_prefetch=4,
        grid=(n_units, n_ff),
        in_specs=[
            pl.BlockSpec(memory_space=pl.ANY),
            wspec((None, None, d, ff), lambda e, j: (layer, e, 0, j)),
            wspec((None, None, d, ff), lambda e, j: (layer, e, 0, n_ff + j)),
            wspec((None, None, 1, ff), lambda e, j: (layer, e, 0, j)),
            wspec((None, None, 1, ff), lambda e, j: (layer, e, 0, n_ff + j)),
            wspec((None, None, ff, d), lambda e, j: (layer, e, j, 0)),
            wspec((None, None, 1, d), lambda e, j: (layer, e, 0, 0)),
        ],
        out_specs=pl.BlockSpec(memory_space=pl.ANY),
        scratch_shapes=[pltpu.VMEM((max_r, d), BF16), pltpu.VMEM((2, MOE_SB, d), F32), pltpu.VMEM((max_r, d), F32),
                        pltpu.VMEM((d, ff), BF16), pltpu.VMEM((d, ff), BF16), pltpu.VMEM((ff, d), BF16),
                        pltpu.SemaphoreType.DMA((2,)), pltpu.SemaphoreType.DMA(())],
    )
    return pl.pallas_call(
        functools.partial(_expert_kernel, n_ff=n_ff),
        grid_spec=grid_spec,
        out_shape=jax.ShapeDtypeStruct((cap, d), F32),
        input_output_aliases={4: 0},
        compiler_params=_cparams(2), name="expert_ffn",
    )(unit_e, unit_row0, unit_nsb, unit_act, x_buf, w_gu, w_gu, b_gu, b_gu, w_down, b_down)


def routing_metadata(route, cnt, t):
    sb, max_r = MOE_SB, MOE_SB * MOE_MAX_SB
    idx = route[:, :TOP_K]
    rank = route[:, ROUTE_RANK_LANE:ROUTE_RANK_LANE + TOP_K]
    sizes = cnt[0, :N_EXPERTS].astype(jnp.int32)
    padded = (sizes + sb - 1) // sb * sb
    pend = jnp.cumsum(padded)
    pstart = pend - padded
    dest = (pstart[idx] + rank).reshape(-1).astype(jnp.int32)
    cap = t * TOP_K + N_EXPERTS * sb
    units_e = (padded + max_r - 1) // max_r
    uend = jnp.cumsum(units_e)
    ustart = uend - units_e
    n_units = cap // max_r + N_EXPERTS
    u = jnp.arange(n_units, dtype=jnp.int32)
    n_act = uend[-1]
    active = u < n_act
    ue = jnp.minimum(jnp.searchsorted(uend, u, side='right'), N_EXPERTS - 1).astype(jnp.int32)
    ue = jnp.where(active, ue, ue[jnp.maximum(n_act - 1, 0)])
    q = u - ustart[ue]
    row0 = jnp.where(active, pstart[ue] + q * max_r, 0)
    nsb = jnp.where(active, jnp.clip((padded[ue] - q * max_r) // sb, 0, MOE_MAX_SB), 0)
    return dest, cap, (ue, row0.astype(jnp.int32), nsb.astype(jnp.int32), active.astype(jnp.int32))


def _combine_kernel(dest, y_hbm, gate_ref, x_ref, gf_ref, lng_ref, lnb_ref, o_ref, ybuf, sem, *, alpha):
    i = pl.program_id(0)
    n = pl.num_programs(0)
    tm = x_ref.shape[0]

    def gather(blk, slot):
        for t in range(tm):
            for k in range(TOP_K):
                row = dest[(blk * tm + t) * TOP_K + k]
                pltpu.make_async_copy(y_hbm.at[pl.ds(row, 1)], ybuf.at[slot, k, pl.ds(t, 1)], sem.at[slot]).start()

    @pl.when(i == 0)
    def _():
        gather(0, 0)

    @pl.when(i + 1 < n)
    def _():
        gather(i + 1, (i + 1) % 2)

    slot = i % 2
    for k in range(TOP_K):
        pltpu.make_async_copy(y_hbm.at[pl.ds(0, tm)], ybuf.at[slot, k], sem.at[slot]).wait()
    gate = gate_ref[...]
    f = gate[:, 0:1] * ybuf[slot, 0]
    for k in range(1, TOP_K):
        f = f + gate[:, k:k + 1] * ybuf[slot, k]
    y = alpha * x_ref[...] + gf_ref[...] * f
    o_ref[...] = _layer_norm_rows(y) * lng_ref[...] + lnb_ref[...]


def combine_block(y_buf, dest, gate, x, mods, layer, rows_per_cond, cond_row0, ln_g, ln_b, alpha):
    rows, d = x.shape
    tm = COMBINE_TM
    row_of = lambda i, dst: cond_row0 + (i * tm) // rows_per_cond
    vec = pl.BlockSpec((1, d), lambda i, dst: (0, 0))
    grid_spec = pltpu.PrefetchScalarGridSpec(
        num_scalar_prefetch=1, grid=(rows // tm,),
        in_specs=[pl.BlockSpec(memory_space=pl.ANY),
                  pl.BlockSpec((tm, LANE), lambda i, dst: (i, 0)),
                  pl.BlockSpec((tm, d), lambda i, dst: (i, 0)),
                  _mod_spec(layer, 5, row_of), vec, vec],
        out_specs=pl.BlockSpec((tm, d), lambda i, dst: (i, 0)),
        scratch_shapes=[pltpu.VMEM((2, TOP_K, tm, d), F32), pltpu.SemaphoreType.DMA((2,))])
    return pl.pallas_call(
        functools.partial(_combine_kernel, alpha=alpha), grid_spec=grid_spec,
        out_shape=jax.ShapeDtypeStruct((rows, d), F32),
        compiler_params=_cparams(1), name="combine",
    )(dest, y_buf, gate, x, mods, ln_g, ln_b)


def _rope_tables(n_tokens):
    pos = jnp.arange(n_tokens)
    row = (pos // GRID_W).astype(F32)
    col = (pos % GRID_W).astype(F32)

    def angles(rot_dim):
        n_freq = rot_dim // 4
        inv = 1.0 / (ROPE_BASE ** (jnp.arange(n_freq, dtype=F32) / n_freq))
        ang = jnp.concatenate([row[:, None] * inv, col[:, None] * inv], -1)
        return jnp.cos(ang), jnp.sin(ang)

    c, s = angles(HEAD_DIM)
    rope_ab = (jnp.concatenate([c, c], -1), jnp.concatenate([-s, s], -1))
    c, s = angles(C_ROPE)
    one = jnp.ones((n_tokens, LANE - C_ROPE), F32)
    zero32 = jnp.zeros_like(s)
    zero64 = jnp.zeros((n_tokens, LANE - C_ROPE), F32)
    rope_c = (jnp.concatenate([c, c, one], -1), jnp.concatenate([-s, zero32, zero64], -1),
              jnp.concatenate([zero32, s, zero64], -1))
    return rope_ab, rope_c


def _group_layer(x, layer, lam_init, alpha, mods, cond_row0, n_batch, n_seq, wts, prm, caches, ropes):
    rows = n_batch * n_seq
    latent = caches is not None
    rope_ab, rope_c = ropes if latent else (None, None)

    rpc = n_seq if latent else rows
    z = ln_mod_matmul(x, mods, layer, 0, rpc, cond_row0, wts['w_main'], mode="rope" if latent else "plain",
                      rope=rope_ab, n_seq=n_seq)
    gates = ln_mod_matmul(x, mods, layer, 0, rpc, cond_row0, wts['w_gates'], mode="sigmoid", n_seq=n_seq)

    tq = min(512, n_seq)
    nqb = n_seq // tq
    zrow = lambda w, col: pl.BlockSpec((n_seq, w), lambda b, h, i: (b, col(h)))
    zq = lambda w, col: pl.BlockSpec((tq, w), lambda b, h, i: (b * nqb + i, col(h)))

    if latent:
        ck = caches['a_k'].reshape(caches['a_k'].shape[:3] + (A_KV_HEADS * HEAD_DIM,))
        cv = caches['a_v'].reshape(ck.shape)
        oa = banded_attention(z, ck, cv, layer, prm['sink'], n_batch, n_seq)
    else:
        oa = attention(
            (n_batch, A_HEADS, nqb),
            [(z, zq(HEAD_DIM, lambda h: COL_QA // HEAD_DIM + h))],
            [[(z, zrow(HEAD_DIM, lambda h: COL_KA // HEAD_DIM + h // A_GROUP))]],
            [(z, zrow(HEAD_DIM, lambda h: COL_VA // HEAD_DIM + h // A_GROUP))],
            zq(HEAD_DIM, lambda h: h), jax.ShapeDtypeStruct((rows, A_HEADS * HEAD_DIM), BF16),
            scale=HEAD_DIM ** -0.5, sink=prm['sink'], name="attn_a_ctx")

    qb0, kb0 = COL_QB // B_HEAD_DIM, COL_KB // B_HEAD_DIM
    k_ops = [[(z, zrow(B_HEAD_DIM, lambda h, m=m: kb0 + m * B_HEADS + h)) for m in range(2)]]
    v_ops = [(z, zrow(B_V_DIM, lambda h: COL_VB // B_V_DIM + h))]
    if latent:
        l_ctx = caches['b_k'].shape[2]
        cbk = caches['b_k'].reshape(caches['b_k'].shape[:3] + (2 * B_HEADS * B_HEAD_DIM,))
        cbv = caches['b_v'].reshape(caches['b_v'].shape[:3] + (B_HEADS * B_V_DIM,))
        cspec = lambda w, col: pl.BlockSpec((None, None, l_ctx, w), lambda b, h, i: (b, layer, 0, col(h)))
        k_ops = [[(cbk, cspec(B_HEAD_DIM, lambda h, m=m: m * B_HEADS + h)) for m in range(2)]] + k_ops
        v_ops = [(cbv, cspec(B_V_DIM, lambda h: h))] + v_ops
    vec3 = lambda a: (a.reshape(a.shape[0], 1, a.shape[1]),
                      pl.BlockSpec((None, 1, a.shape[1]), lambda b, h, i: (layer, 0, 0)))
    ob = attention(
        (n_batch, B_HEADS, nqb),
        [(z, zq(B_HEAD_DIM, lambda h, m=m: qb0 + m * B_HEADS + h)) for m in range(2)],
        k_ops, v_ops, zq(B_V_DIM, lambda h: h), jax.ShapeDtypeStruct((rows, B_HEADS * B_V_DIM), BF16),
        scale=B_HEAD_DIM ** -0.5,
        diff_ops=[vec3(prm['lam_q1']), vec3(prm['lam_k1']), vec3(prm['lam_q2']), vec3(prm['lam_k2']),
                  vec3(prm['subln'])],
        lam_init=lam_init, name="attn_b")

    q_c = c_query(z, prm['q_norm'], wts['w_uq'], rope_c, n_seq)
    tmc = min(512, n_seq if latent else rows)
    per_seq = max(n_seq // tmc, 1)
    kv_out = c_keyvalue(
        (z, pl.BlockSpec((tmc, C_KV_LORA), lambda i: (i, COL_CKV // C_KV_LORA))),
        (z, pl.BlockSpec((tmc, LANE), lambda i: (i, COL_KR // LANE))),
        rows, tmc, (rows // tmc,), prm['kv_norm'], wts['w_uk'], wts['w_uv'],
        rope_c=rope_c, rope_spec=pl.BlockSpec((tmc, LANE), lambda i: (i % per_seq, 0)), emit_norm=not latent)
    k_c, v_c = kv_out[0], kv_out[1]
    crow = lambda w: pl.BlockSpec((n_seq, w), lambda b, h, i: (b, h))
    cq_spec = lambda w: pl.BlockSpec((tq, w), lambda b, h, i: (b * nqb + i, h))
    k_ops = [[(k_c, crow(C_QK_PAD))]]
    v_ops = [(v_c, crow(C_VDIM))]
    if latent:
        l_ctx = caches['c_kv'].shape[2]
        kx, vx = c_keyvalue(
            (caches['c_kv'], pl.BlockSpec((None, None, l_ctx, C_KV_LORA), lambda b, s: (b, layer, 0, 0))),
            (caches['c_kr'], pl.BlockSpec((None, None, l_ctx, LANE), lambda b, s: (b, layer, 0, 0))),
            n_batch * l_ctx, l_ctx, (n_batch, 1), None, wts['w_uk'], wts['w_uv'])
        xrow = lambda w: pl.BlockSpec((l_ctx, w), lambda b, h, i: (b, h))
        k_ops = [[(kx, xrow(C_QK_PAD))]] + k_ops
        v_ops = [(vx, xrow(C_VDIM))] + v_ops
    oc = attention(
        (n_batch, C_HEADS, nqb), [(q_c, cq_spec(C_QK_PAD))], k_ops, v_ops,
        cq_spec(C_VDIM), jax.ShapeDtypeStruct((rows, C_HEADS * C_VDIM), BF16),
        scale=(C_NOPE + C_ROPE) ** -0.5, name="attn_c")

    x = merge_block(oa, ob, oc, gates, x, mods, layer, rpc, cond_row0, prm['ln1_g'], prm['ln1_b'],
                    wts['w_br_a'], wts['w_br_b'], wts['w_br_c'], wts['w_out'], alpha)

    h, route, gate, cnt = router(x, mods, layer, rpc, cond_row0, wts['w_router'], prm['b_router'])
    dest, cap, units = routing_metadata(route, cnt, rows)
    x_buf = dispatch_rows(h, dest, cap)
    y_buf = expert_ffn(x_buf, units, layer, wts['w_gu'], wts['b_gu'], wts['w_down'], wts['b_down'])
    x = combine_block(y_buf, dest, gate, x, mods, layer, rpc, cond_row0, prm['ln2_g'], prm['ln2_b'], alpha)

    state = None
    if not latent:
        state = (z[:, COL_KA:COL_KA + 256], z[:, COL_VA:COL_VA + 256], z[:, COL_KB:COL_KB + 1024],
                 z[:, COL_VB:COL_VB + 1024], kv_out[2], z[:, COL_KR:COL_KR + C_ROPE])
    return x, state


def kernel(x_prompt, x_sample, c, cache_a_k, cache_a_v, cache_b_k, cache_b_v, cache_c_kv, cache_c_kr, c_ctx, w_ada, b_ada, w_in, sink_a, lam_q1, lam_k1, lam_q2, lam_k2, subln_b, q_norm_c, kv_norm_c, w_uq, w_ukv, w_br_a, w_br_b, w_br_c, w_out, ln1_g, ln1_b, ln2_g, ln2_b, w_router, b_router, w_gu, b_gu, w_down, b_down):
    depth = w_in.shape[0]
    batch, seq, d = x_prompt.shape
    dec_batch, dec_seq, _ = x_sample.shape
    alpha = (2 * depth) ** 0.25
    assert d == D_MODEL and dec_batch + 1 <= 8

    cond8 = jnp.zeros((8, d), F32).at[0].set(c_ctx).at[1:1 + dec_batch].set(c)
    mods = adaln_all(cond8, w_ada, b_ada).reshape(depth, 8, 6, 1, d)

    ropes = _rope_tables(dec_seq)
    n_exp = w_gu.shape[1]
    b_gu4 = b_gu.reshape(depth, n_exp, 1, 2 * D_FF)
    b_down4 = b_down.reshape(depth, n_exp, 1, d)
    caches = dict(a_k=cache_a_k, a_v=cache_a_v, b_k=cache_b_k, b_v=cache_b_v, c_kv=cache_c_kv,
                  c_kr=jnp.pad(cache_c_kr, ((0, 0), (0, 0), (0, 0), (0, LANE - C_ROPE))))

    xp = x_prompt.reshape(batch * seq, d)
    xs = x_sample.reshape(dec_batch * dec_seq, d)
    states = []
    for l in range(depth):
        lam_init = 0.8 - 0.6 * math.exp(-0.3 * l)
        wl = w_in[l]
        w_main = jnp.concatenate([wl[:, :N_IN_SRC], jnp.zeros((d, N_MAIN - N_IN_SRC), F32)], axis=1).astype(BF16)
        w_uq_l = w_uq[l].reshape(C_Q_LORA, C_HEADS, C_NOPE + C_ROPE)
        w_uq_p = jnp.pad(w_uq_l, ((0, 0), (0, 0), (0, C_QK_PAD - C_NOPE - C_ROPE))).reshape(C_Q_LORA, -1).astype(BF16)
        w_ukv_l = w_ukv[l].reshape(C_KV_LORA, C_HEADS, C_NOPE + C_VDIM)
        wts = dict(
            w_main=w_main, w_gates=wl[:, N_IN_SRC:].astype(BF16), w_uq=w_uq_p,
            w_uk=w_ukv_l[:, :, :C_NOPE].reshape(C_KV_LORA, -1).astype(BF16),
            w_uv=w_ukv_l[:, :, C_NOPE:].reshape(C_KV_LORA, -1).astype(BF16),
            w_br_a=w_br_a[l].astype(BF16), w_br_b=w_br_b[l].astype(BF16), w_br_c=w_br_c[l].astype(BF16),
            w_out=w_out[l].astype(BF16),
            w_router=jnp.pad(w_router[l], ((0, 0), (0, LANE - n_exp))),
            w_gu=w_gu, b_gu=b_gu4, w_down=w_down, b_down=b_down4)
        prm = dict(sink=sink_a[l], lam_q1=lam_q1, lam_k1=lam_k1, lam_q2=lam_q2, lam_k2=lam_k2, subln=subln_b,
                   q_norm=q_norm_c[l][None], kv_norm=kv_norm_c[l][None],
                   ln1_g=ln1_g[l][None], ln1_b=ln1_b[l][None], ln2_g=ln2_g[l][None], ln2_b=ln2_b[l][None],
                   b_router=jnp.pad(b_router[l], (0, LANE - n_exp))[None])
        xp, st = _group_layer(xp, l, lam_init, alpha, mods, 0, batch, seq, wts, prm, None, None)
        xs, _ = _group_layer(xs, l, lam_init, alpha, mods, 1, dec_batch, dec_seq, wts, prm, caches, ropes)
        states.append(st)

    def stack(i, shape):
        return jnp.stack([s[i].reshape((batch, seq) + shape) for s in states], axis=1)

    return (xp.reshape(batch, seq, d), xs.reshape(dec_batch, dec_seq, d),
            stack(0, (A_KV_HEADS, HEAD_DIM)), stack(1, (A_KV_HEADS, HEAD_DIM)),
            stack(2, (2, B_HEADS, B_HEAD_DIM)), stack(3, (B_HEADS, B_V_DIM)),
            stack(4, (C_KV_LORA,)), stack(5, (C_ROPE,)))
```

```python
import functools
import math

import jax
import jax.numpy as jnp
from jax import lax
from jax.experimental import pallas as pl
from jax.experimental.pallas import tpu as pltpu

F32 = jnp.float32
BF16 = jnp.bfloat16

D_MODEL = 2048
GRID_W = 64
WINDOW = 128
A_HEADS = 8
A_KV_HEADS = 2
A_GROUP = A_HEADS // A_KV_HEADS
HEAD_DIM = 128
B_HEADS = 4
B_HEAD_DIM = 128
B_V_DIM = 2 * B_HEAD_DIM
C_HEADS = 8
C_NOPE = 128
C_ROPE = 64
C_VDIM = 128
C_Q_LORA = 512
C_KV_LORA = 256
C_QK_PAD = 256
N_BRANCH = 3
N_EXPERTS = 32
TOP_K = 4
D_FF = 2048
SWIGLU_LIMIT = 7.0
SWIGLU_ALPHA = 1.702
ROPE_BASE = 10000.0
LN_EPS = 1e-5
RMS_EPS = 1e-6
NEG_INF = -1e30

COL_QA = 0
COL_KA = 1024
COL_VA = 1280
COL_QB = 1536
COL_KB = 2560
COL_VB = 3584
COL_CQ = 4608
COL_CKV = 5120
COL_KR = 5376
N_MAIN = 5632
N_GATES = N_BRANCH * D_MODEL
N_IN_SRC = COL_KR + C_ROPE

LANE = 128
V7X_VMEM_LIMIT = 56 * 1024 * 1024

LOG2_E = 1.4426950408889634

PROJ_TM = 1024
PROJ_TN = 512
ATTN_SUB = 256
MERGE_TM = 256
ROUTER_TM = 256
MOE_SB = 256
MOE_MAX_SB = 8
MOE_FF = 256
DISPATCH_TM = 256
COMBINE_TM = 128


def _cparams(n_grid):
    return pltpu.CompilerParams(dimension_semantics=("arbitrary",) * n_grid, vmem_limit_bytes=V7X_VMEM_LIMIT)


def _sigmoid(x):
    return 1.0 / (1.0 + jnp.exp(-x))


def _layer_norm_rows(x):
    mu = jnp.mean(x, axis=-1, keepdims=True)
    xc = x - mu
    var = jnp.mean(xc * xc, axis=-1, keepdims=True)
    return xc * lax.rsqrt(var + LN_EPS)


def _rms_rows(x, g):
    return x * lax.rsqrt(jnp.mean(x * x, axis=-1, keepdims=True) + RMS_EPS) * g


def _rope_half128(x, cosf, sinf):
    return x * cosf + pltpu.roll(x, 64, 1) * sinf


def _rope_pad64(x, cf, sa, sb):
    return x * cf + pltpu.roll(x, 96, 1) * sa + pltpu.roll(x, 32, 1) * sb


def _adaln_kernel(c_ref, w_ref, b_ref, o_ref):
    c = c_ref[...]
    s = (c * _sigmoid(c)).astype(BF16)
    o_ref[...] = jnp.dot(s, w_ref[...].astype(BF16), preferred_element_type=F32) + b_ref[...]


def adaln_all(cond8, w_ada, b_ada):
    depth, d, n = w_ada.shape
    tn = 1024
    return pl.pallas_call(
        _adaln_kernel,
        grid=(depth, n // tn),
        in_specs=[
            pl.BlockSpec((8, d), lambda l, j: (0, 0)),
            pl.BlockSpec((None, d, tn), lambda l, j: (l, 0, j)),
            pl.BlockSpec((None, 1, tn), lambda l, j: (l, 0, j)),
        ],
        out_specs=pl.BlockSpec((None, 8, tn), lambda l, j: (l, 0, j)),
        out_shape=jax.ShapeDtypeStruct((depth, 8, n), F32),
        compiler_params=_cparams(2),
        name="adaln",
    )(cond8, w_ada, b_ada.reshape(depth, 1, n))


def _mod_spec(layer, slot, row_of_tile):
    return pl.BlockSpec((None, None, None, 1, D_MODEL), lambda *g: (layer, row_of_tile(*g), slot, 0, 0))


ROPE_FULL_TILES = (0, 1, 3, 4, 5, 6)
ROPE_HALF_TILE = 2


def _lnmm_kernel(*refs, mode):
    if mode == "rope":
        x_ref, sc_ref, sh_ref, w_ref, cos_ref, sin_ref, o_ref, h_scr = refs
    else:
        x_ref, sc_ref, sh_ref, w_ref, o_ref, h_scr = refs
    j = pl.program_id(1)

    @pl.when(j == 0)
    def _():
        h = _layer_norm_rows(x_ref[...]) * (1.0 + sc_ref[...]) + sh_ref[...]
        h_scr[...] = h.astype(BF16)

    acc = jnp.dot(h_scr[...], w_ref[...], preferred_element_type=F32)
    if mode == "sigmoid":
        o_ref[...] = _sigmoid(acc)
    elif mode == "plain":
        o_ref[...] = acc
    else:
        is_full = functools.reduce(jnp.logical_or, [j == t for t in ROPE_FULL_TILES])
        is_half = j == ROPE_HALF_TILE

        def rotated(n_heads):
            cosf, sinf = cos_ref[...], sin_ref[...]
            for hh in range(PROJ_TN // HEAD_DIM):
                blk = acc[:, hh * HEAD_DIM:(hh + 1) * HEAD_DIM]
                if hh < n_heads:
                    blk = _rope_half128(blk, cosf, sinf)
                o_ref[:, hh * HEAD_DIM:(hh + 1) * HEAD_DIM] = blk

        @pl.when(is_full)
        def _():
            rotated(PROJ_TN // HEAD_DIM)

        @pl.when(is_half)
        def _():
            rotated(A_KV_HEADS)

        @pl.when(jnp.logical_not(jnp.logical_or(is_full, is_half)))
        def _():
            o_ref[...] = acc


def ln_mod_matmul(x, x_row0, rows, mods, layer, slot, rows_per_cond, cond_row0, w, *, mode, rope=None, n_seq=None):
    d = x.shape[1]
    n = w.shape[1]
    tm = min(PROJ_TM, n_seq if mode == "rope" else rows_per_cond)
    tn = PROJ_TN
    off = x_row0 // tm
    assert x_row0 % tm == 0 and rows % tm == 0
    row_of = lambda i, j: cond_row0 + (i * tm) // rows_per_cond
    in_specs = [
        pl.BlockSpec((tm, d), lambda i, j: (i + off, 0)),
        _mod_spec(layer, slot + 1, row_of),
        _mod_spec(layer, slot, row_of),
        pl.BlockSpec((d, tn), lambda i, j: (0, j)),
    ]
    args = [x, mods, mods, w]
    if mode == "rope":
        per_seq = n_seq // tm
        in_specs += [pl.BlockSpec((tm, LANE), lambda i, j: (i % per_seq, 0))] * 2
        args += list(rope)
    return pl.pallas_call(
        functools.partial(_lnmm_kernel, mode=mode),
        grid=(rows // tm, n // tn),
        in_specs=in_specs,
        out_specs=pl.BlockSpec((tm, tn), lambda i, j: (i, j)),
        out_shape=jax.ShapeDtypeStruct((rows, n), F32),
        scratch_shapes=[pltpu.VMEM((tm, d), BF16)],
        compiler_params=_cparams(2),
        name="ln_mod_matmul_" + mode,
    )(*args)


def _attn_kernel(*refs, n_maps, n_seg, scale, use_sink, diff, lam_init):
    pos = 0
    if use_sink:
        sink_ref = refs[0]
        pos = 1
    q_refs = refs[pos:pos + n_maps]
    pos += n_maps
    k_refs = [refs[pos + sg * n_maps: pos + (sg + 1) * n_maps] for sg in range(n_seg)]
    pos += n_seg * n_maps
    v_refs = refs[pos:pos + n_seg]
    pos += n_seg
    if diff:
        lq1, lk1, lq2, lk2, subln_ref = refs[pos:pos + 5]
        pos += 5
    o_ref = refs[pos]

    c2 = scale * LOG2_E
    ks = [[k_refs[sg][m][...].astype(BF16) for m in range(n_maps)] for sg in range(n_seg)]
    vs = [v_refs[sg][...].astype(BF16) for sg in range(n_seg)]
    if use_sink:
        sink_raw = sink_ref[pl.program_id(1)] * (1.0 / scale)
    if diff:
        f = jnp.sum(lq1[...] * lk1[...], axis=-1, keepdims=True)
        g = jnp.sum(lq2[...] * lk2[...], axis=-1, keepdims=True)
        lam = jnp.exp(f) - jnp.exp(g) + lam_init
    tq = o_ref.shape[0]
    sub = min(tq, ATTN_SUB)
    for r0 in range(0, tq, sub):
        es, dens = [], []
        for m in range(n_maps):
            q = q_refs[m][r0:r0 + sub, :].astype(BF16)
            s = [lax.dot_general(q, ks[sg][m], (((1,), (1,)), ((), ())), preferred_element_type=F32)
                 for sg in range(n_seg)]
            mx = functools.reduce(jnp.maximum, [jnp.max(t, axis=-1, keepdims=True) for t in s])
            if use_sink:
                mx = jnp.maximum(mx, sink_raw)
            e = [jnp.exp2((t - mx) * c2) for t in s]
            den = functools.reduce(jnp.add, [jnp.sum(t, axis=-1, keepdims=True) for t in e])
            if use_sink:
                den = den + jnp.exp2((sink_raw - mx) * c2)
            es.append(e)
            dens.append(den)
        if diff:
            coef = lam * dens[0] * (1.0 / dens[1])
            a = [es[0][sg] - coef * es[1][sg] for sg in range(n_seg)]
        else:
            a = es[0]
        o = functools.reduce(jnp.add, [jnp.dot(a[sg].astype(BF16), vs[sg], preferred_element_type=F32)
                                       for sg in range(n_seg)])
        o = o * (1.0 / dens[0])
        if diff:
            o = _rms_rows(o, subln_ref[...]) * (1.0 - lam_init)
        o_ref[r0:r0 + sub, :] = o.astype(o_ref.dtype)


def attention(grid, q_ops, k_ops, v_ops, out_spec, out_shape, *, scale, sink=None, diff_ops=None, lam_init=0.0,
              name="attn"):
    n_maps, n_seg = len(q_ops), len(k_ops)
    ops = []
    if sink is not None:
        ops.append((sink, pl.BlockSpec(memory_space=pltpu.SMEM)))
    ops += list(q_ops)
    for seg in k_ops:
        ops += list(seg)
    ops += list(v_ops)
    if diff_ops is not None:
        ops += list(diff_ops)
    kern = functools.partial(_attn_kernel, n_maps=n_maps, n_seg=n_seg, scale=scale, use_sink=sink is not None,
                             diff=diff_ops is not None, lam_init=lam_init)
    return pl.pallas_call(
        kern, grid=grid,
        in_specs=[s for _, s in ops],
        out_specs=out_spec, out_shape=out_shape,
        compiler_params=_cparams(len(grid)), name=name,
    )(*[a for a, _ in ops])


def _band_kernel(sink_ref, q_ref, kp_ref, kc_ref, kn_ref, kx_ref, vp_ref, vc_ref, vn_ref, vx_ref, o_ref, *, n_blocks):
    g = pl.program_id(1)
    i = pl.program_id(2)
    w = WINDOW
    q = q_ref[...]
    qs = jnp.concatenate([q[:, r * HEAD_DIM:(r + 1) * HEAD_DIM] for r in range(A_GROUP)], axis=0).astype(BF16)
    kcat = jnp.concatenate([kp_ref[...], kc_ref[...], kn_ref[...], kx_ref[...]], axis=0).astype(BF16)
    vcat = jnp.concatenate([vp_ref[...], vc_ref[...], vn_ref[...], vx_ref[...]], axis=0).astype(BF16)
    s = lax.dot_general(qs, kcat, (((1,), (1,)), ((), ())), preferred_element_type=F32) * (HEAD_DIM ** -0.5)
    n_keys = s.shape[1]
    row = lax.broadcasted_iota(jnp.int32, s.shape, 0) & (w - 1)
    col = lax.broadcasted_iota(jnp.int32, s.shape, 1)
    prev_slack = jnp.where(i > 0, col - row, -1)
    next_slack = jnp.where(i < n_blocks - 1, row - col + 2 * w, -1)
    slack = jnp.where(col < w, prev_slack, jnp.where(col < 2 * w, 0, jnp.where(col < 3 * w, next_slack, 0)))
    s = jnp.where(slack >= 0, s, NEG_INF)
    sink = jnp.concatenate([jnp.full((w, 1), sink_ref[g * A_GROUP + r], F32) for r in range(A_GROUP)], axis=0)
    mx = jnp.maximum(jnp.max(s, axis=-1, keepdims=True), sink)
    e = jnp.exp(s - mx)
    den = jnp.sum(e, axis=-1, keepdims=True) + jnp.exp(sink - mx)
    p = (e * (1.0 / den)).astype(BF16)
    o = jnp.dot(p, vcat, preferred_element_type=F32)
    for r in range(A_GROUP):
        o_ref[:, r * HEAD_DIM:(r + 1) * HEAD_DIM] = o[r * w:(r + 1) * w].astype(o_ref.dtype)


def banded_attention(z, cache_k, cache_v, layer, sink, n_batch, n_seq):
    nb = n_seq // WINDOW
    l_ctx = cache_k.shape[2]
    qw = A_GROUP * HEAD_DIM
    kcol, vcol = COL_KA // HEAD_DIM, COL_VA // HEAD_DIM

    def blk(col0, shift):
        def imap(b, g, i):
            return (b * nb + jnp.clip(i + shift, 0, nb - 1), col0 + g)
        return pl.BlockSpec((WINDOW, HEAD_DIM), imap)

    ctx_spec = pl.BlockSpec((None, None, l_ctx, HEAD_DIM), lambda b, g, i: (b, layer, 0, g))
    return pl.pallas_call(
        functools.partial(_band_kernel, n_blocks=nb),
        grid=(n_batch, A_KV_HEADS, nb),
        in_specs=[pl.BlockSpec(memory_space=pltpu.SMEM),
                  pl.BlockSpec((WINDOW, qw), lambda b, g, i: (b * nb + i, g)),
                  blk(kcol, -1), blk(kcol, 0), blk(kcol, 1), ctx_spec,
                  blk(vcol, -1), blk(vcol, 0), blk(vcol, 1), ctx_spec],
        out_specs=pl.BlockSpec((WINDOW, qw), lambda b, g, i: (b * nb + i, g)),
        out_shape=jax.ShapeDtypeStruct((n_batch * n_seq, A_HEADS * HEAD_DIM), BF16),
        compiler_params=_cparams(3), name="banded_attn",
    )(sink, z, z, z, z, cache_k, z, z, z, cache_v)


def _cq_kernel(*refs, rope):
    if rope:
        cq_ref, g_ref, w_ref, cf_ref, sa_ref, sb_ref, o_ref = refs
    else:
        cq_ref, g_ref, w_ref, o_ref = refs
    y = _rms_rows(cq_ref[...], g_ref[...]).astype(BF16)
    q = jnp.dot(y, w_ref[...], preferred_element_type=F32)
    if rope:
        cf, sa, sb = cf_ref[...], sa_ref[...], sb_ref[...]
    for h in range(C_HEADS):
        lo = h * C_QK_PAD
        o_ref[:, lo:lo + C_NOPE] = q[:, lo:lo + C_NOPE].astype(o_ref.dtype)
        blk = q[:, lo + C_NOPE:lo + C_QK_PAD]
        if rope:
            blk = _rope_pad64(blk, cf, sa, sb)
        o_ref[:, lo + C_NOPE:lo + C_QK_PAD] = blk.astype(o_ref.dtype)


def c_query(z, q_norm, w_uq_p, rope_c, n_seq):
    rows = z.shape[0]
    rope = rope_c is not None
    tm = min(512, n_seq if rope else rows)
    in_specs = [pl.BlockSpec((tm, C_Q_LORA), lambda i: (i, COL_CQ // C_Q_LORA)),
                pl.BlockSpec((1, C_Q_LORA), lambda i: (0, 0)),
                pl.BlockSpec(w_uq_p.shape, lambda i: (0, 0))]
    args = [z, q_norm, w_uq_p]
    if rope:
        per_seq = n_seq // tm
        in_specs += [pl.BlockSpec((tm, LANE), lambda i: (i % per_seq, 0))] * 3
        args += list(rope_c)
    return pl.pallas_call(
        functools.partial(_cq_kernel, rope=rope),
        grid=(rows // tm,), in_specs=in_specs,
        out_specs=pl.BlockSpec((tm, C_HEADS * C_QK_PAD), lambda i: (i, 0)),
        out_shape=jax.ShapeDtypeStruct((rows, C_HEADS * C_QK_PAD), BF16),
        compiler_params=_cparams(1), name="c_query",
    )(*args)


def _ckv_kernel(*refs, normalize, rope, emit_norm):
    refs = list(refs)
    ckv_ref, kr_ref = refs[:2]
    pos = 2
    if normalize:
        g_ref = refs[pos]
        pos += 1
    wn_ref, wv_ref = refs[pos:pos + 2]
    pos += 2
    if rope:
        cf_ref, sa_ref, sb_ref = refs[pos:pos + 3]
        pos += 3
    k_ref, v_ref = refs[pos:pos + 2]
    pos += 2
    ckv = ckv_ref[...]
    if normalize:
        ckv = _rms_rows(ckv, g_ref[...])
    if emit_norm:
        refs[pos][...] = ckv
    cb = ckv.astype(BF16)
    kn = jnp.dot(cb, wn_ref[...], preferred_element_type=F32)
    v_ref[...] = jnp.dot(cb, wv_ref[...], preferred_element_type=F32).astype(v_ref.dtype)
    kr = kr_ref[...]
    if rope:
        kr = _rope_pad64(kr, cf_ref[...], sa_ref[...], sb_ref[...])
    kr = kr.astype(k_ref.dtype)
    for h in range(C_HEADS):
        lo = h * C_QK_PAD
        k_ref[:, lo:lo + C_NOPE] = kn[:, h * C_NOPE:(h + 1) * C_NOPE].astype(k_ref.dtype)
        k_ref[:, lo + C_NOPE:lo + C_QK_PAD] = kr


def c_keyvalue(ckv_op, kr_op, rows, tm, grid, kv_norm, w_nope, w_v, rope_c=None, rope_spec=None, emit_norm=False):
    ops = [ckv_op, kr_op]
    full = lambda a: (a, pl.BlockSpec(a.shape, lambda *g: (0,) * a.ndim))
    if kv_norm is not None:
        ops.append(full(kv_norm))
    ops += [full(w_nope), full(w_v)]
    if rope_c is not None:
        ops += [(t, rope_spec) for t in rope_c]
    row_map = lambda *g: (functools.reduce(lambda a, b: a * grid[1] + b, g) if len(g) > 1 else g[0], 0)
    out_specs = [pl.BlockSpec((tm, C_HEADS * C_QK_PAD), row_map), pl.BlockSpec((tm, C_HEADS * C_VDIM), row_map)]
    out_shape = [jax.ShapeDtypeStruct((rows, C_HEADS * C_QK_PAD), BF16),
                 jax.ShapeDtypeStruct((rows, C_HEADS * C_VDIM), BF16)]
    if emit_norm:
        out_specs.append(pl.BlockSpec((tm, C_KV_LORA), row_map))
        out_shape.append(jax.ShapeDtypeStruct((rows, C_KV_LORA), F32))
    return pl.pallas_call(
        functools.partial(_ckv_kernel, normalize=kv_norm is not None, rope=rope_c is not None, emit_norm=emit_norm),
        grid=grid, in_specs=[s for _, s in ops], out_specs=out_specs, out_shape=out_shape,
        compiler_params=_cparams(len(grid)), name="c_keyvalue",
    )(*[a for a, _ in ops])


def _merge_kernel(oa_ref, ob_ref, oc_ref, g_ref, x_ref, ga_ref, lng_ref, lnb_ref, wa_ref, wb_ref, wc_ref, wo_ref,
                  o_ref, mix_scr, *, alpha):
    d = D_MODEL
    chunk = 512
    for c in range(d // chunk):
        sl = slice(c * chunk, (c + 1) * chunk)
        ya = jnp.dot(oa_ref[...], wa_ref[:, sl], preferred_element_type=F32)
        yb = jnp.dot(ob_ref[...], wb_ref[:, sl], preferred_element_type=F32)
        yc = jnp.dot(oc_ref[...], wc_ref[:, sl], preferred_element_type=F32)
        mixed = (g_ref[:, c * chunk:(c + 1) * chunk] * ya
                 + g_ref[:, d + c * chunk:d + (c + 1) * chunk] * yb
                 + g_ref[:, 2 * d + c * chunk:2 * d + (c + 1) * chunk] * yc)
        mix_scr[:, sl] = mixed.astype(BF16)
    out = jnp.dot(mix_scr[...], wo_ref[...], preferred_element_type=F32)
    y = alpha * x_ref[...] + ga_ref[...] * out
    o_ref[...] = _layer_norm_rows(y) * lng_ref[...] + lnb_ref[...]


def merge_block(oa, ob, oc, gates, x, x_row0, mods, layer, rows_per_cond, cond_row0, ln_g, ln_b, wa, wb, wc, wo,
                alpha):
    rows, d = oa.shape[0], x.shape[1]
    tm = MERGE_TM
    off = x_row0 // tm
    assert x_row0 % tm == 0
    row_of = lambda i: cond_row0 + (i * tm) // rows_per_cond
    const = lambda a: pl.BlockSpec(a.shape, lambda i: (0, 0), pipeline_mode=pl.Buffered(1))
    rowblk = lambda a: pl.BlockSpec((tm, a.shape[1]), lambda i: (i, 0))
    vec = pl.BlockSpec((1, d), lambda i: (0, 0))
    return pl.pallas_call(
        functools.partial(_merge_kernel, alpha=alpha),
        grid=(rows // tm,),
        in_specs=[rowblk(oa), rowblk(ob), rowblk(oc), rowblk(gates),
                  pl.BlockSpec((tm, d), lambda i: (i + off, 0)), _mod_spec(layer, 2, row_of),
                  vec, vec, const(wa), const(wb), const(wc), const(wo)],
        out_specs=pl.BlockSpec((tm, d), lambda i: (i, 0)),
        out_shape=jax.ShapeDtypeStruct((rows, d), F32),
        scratch_shapes=[pltpu.VMEM((tm, d), BF16)],
        compiler_params=_cparams(1), name="merge",
    )(oa, ob, oc, gates, x, mods, ln_g, ln_b, wa, wb, wc, wo)


ROUTE_RANK_LANE = TOP_K


def _two_group_rows(i, n_ctx_tiles, xc_ref, xl_ref):
    return jnp.where(i < n_ctx_tiles, xc_ref[...], xl_ref[...])


def _two_group_specs(tm, d, n_ctx_tiles, n_lat_tiles):
    return [pl.BlockSpec((tm, d), lambda i, *_: (jnp.minimum(i, n_ctx_tiles - 1), 0)),
            pl.BlockSpec((tm, d), lambda i, *_: (jnp.clip(i - n_ctx_tiles, 0, n_lat_tiles - 1), 0))]


def _two_group_cond_row(tm, rows_ctx, n_seq_lat):
    return lambda i, *_: jnp.where(i * tm < rows_ctx, 0, 1 + (i * tm - rows_ctx) // n_seq_lat)


def _router_kernel(xc_ref, xl_ref, sc_ref, sh_ref, wr_ref, br_ref, h_ref, route_ref, gate_ref, cnt_ref, carry_scr,
                   *, n_ctx_tiles):
    i = pl.program_id(0)

    @pl.when(i == 0)
    def _():
        carry_scr[...] = jnp.zeros_like(carry_scr)

    x = _two_group_rows(i, n_ctx_tiles, xc_ref, xl_ref)
    h = _layer_norm_rows(x) * (1.0 + sc_ref[...]) + sh_ref[...]
    h_ref[...] = h
    logits = jnp.dot(h, wr_ref[...], preferred_element_type=F32, precision=lax.Precision.HIGHEST) + br_ref[...]
    tm = logits.shape[0]
    lane = lax.broadcasted_iota(jnp.int32, (tm, LANE), 1)
    cur = jnp.where(lane < N_EXPERTS, logits, -jnp.inf)
    vals, idxs = [], []
    for _ in range(TOP_K):
        m = jnp.max(cur, axis=-1, keepdims=True)
        ik = jnp.min(jnp.where(cur == m, lane, LANE), axis=-1, keepdims=True)
        vals.append(m)
        idxs.append(ik)
        cur = jnp.where(lane == ik, -jnp.inf, cur)
    e = [jnp.exp(v - vals[0]) for v in vals]
    inv = 1.0 / functools.reduce(jnp.add, e)
    onehot = functools.reduce(jnp.add, [(lane == ik).astype(F32) for ik in idxs])
    r_i = lax.broadcasted_iota(jnp.int32, (tm, tm), 0)
    c_i = lax.broadcasted_iota(jnp.int32, (tm, tm), 1)
    tri = (c_i < r_i).astype(BF16)
    rank_full = jnp.dot(tri, onehot.astype(BF16), preferred_element_type=F32) + carry_scr[...]
    route = jnp.zeros((tm, LANE), jnp.int32)
    gate = jnp.zeros((tm, LANE), F32)
    for k in range(TOP_K):
        rk = jnp.sum(jnp.where(lane == idxs[k], rank_full, 0.0), axis=-1, keepdims=True).astype(jnp.int32)
        route = jnp.where(lane == k, idxs[k], route)
        route = jnp.where(lane == ROUTE_RANK_LANE + k, rk, route)
        gate = jnp.where(lane == k, e[k] * inv, gate)
    route_ref[...] = route
    gate_ref[...] = gate
    carry_scr[...] = carry_scr[...] + jnp.sum(onehot, axis=0, keepdims=True)
    cnt_ref[...] = carry_scr[...]


def router(x_ctx, x_lat, n_seq_lat, mods, layer, w_router_p, b_router_p):
    rows_ctx, d = x_ctx.shape
    rows = rows_ctx + x_lat.shape[0]
    tm = ROUTER_TM
    n_ctx_tiles, n_lat_tiles = rows_ctx // tm, x_lat.shape[0] // tm
    row_of = _two_group_cond_row(tm, rows_ctx, n_seq_lat)
    rowblk = lambda w: pl.BlockSpec((tm, w), lambda i: (i, 0))
    return pl.pallas_call(
        functools.partial(_router_kernel, n_ctx_tiles=n_ctx_tiles), grid=(rows // tm,),
        in_specs=_two_group_specs(tm, d, n_ctx_tiles, n_lat_tiles) + [
            _mod_spec(layer, 4, row_of), _mod_spec(layer, 3, row_of),
            pl.BlockSpec((d, LANE), lambda i: (0, 0)), pl.BlockSpec((1, LANE), lambda i: (0, 0))],
        out_specs=[rowblk(d), rowblk(LANE), rowblk(LANE), pl.BlockSpec((1, LANE), lambda i: (0, 0))],
        out_shape=[jax.ShapeDtypeStruct((rows, d), F32), jax.ShapeDtypeStruct((rows, LANE), jnp.int32),
                   jax.ShapeDtypeStruct((rows, LANE), F32), jax.ShapeDtypeStruct((1, LANE), F32)],
        scratch_shapes=[pltpu.VMEM((1, LANE), F32)],
        compiler_params=_cparams(1), name="router",
    )(x_ctx, x_lat, mods, mods, w_router_p, b_router_p)


def _dispatch_kernel(dest, h_ref, init_hbm, x_hbm, sem):
    del init_hbm
    i = pl.program_id(0)
    tm = h_ref.shape[0]
    for t in range(tm):
        for k in range(TOP_K):
            row = dest[(i * tm + t) * TOP_K + k]
            pltpu.make_async_copy(h_ref.at[pl.ds(t, 1)], x_hbm.at[pl.ds(row, 1)], sem).start()
    for k in range(TOP_K):
        pltpu.make_async_copy(h_ref, x_hbm.at[pl.ds(0, tm)], sem).wait()


def dispatch_rows(h, dest, cap):
    t, d = h.shape
    tm = DISPATCH_TM
    grid_spec = pltpu.PrefetchScalarGridSpec(
        num_scalar_prefetch=1, grid=(t // tm,),
        in_specs=[pl.BlockSpec((tm, d), lambda i, dst: (i, 0)), pl.BlockSpec(memory_space=pl.ANY)],
        out_specs=pl.BlockSpec(memory_space=pl.ANY),
        scratch_shapes=[pltpu.SemaphoreType.DMA(())])
    return pl.pallas_call(
        _dispatch_kernel, grid_spec=grid_spec,
        out_shape=jax.ShapeDtypeStruct((cap, d), F32),
        input_output_aliases={2: 0},
        compiler_params=_cparams(1), name="dispatch",
    )(dest, h, jnp.zeros((cap, d), F32))


def _expert_kernel(unit_e, unit_row0, unit_nsb, unit_act,
                   x_hbm, wg_ref, wu_ref, bg_ref, bu_ref, wd_ref, bd_ref, y_hbm,
                   xs, stage, acc, wgb, wub, wdb, sem_in, sem_out, *, n_ff):
    del unit_e
    u = pl.program_id(0)
    j = pl.program_id(1)
    sb = MOE_SB
    nsb = unit_nsb[u]
    row0 = unit_row0[u]
    active = unit_act[u] == 1

    def rows(rb, base=0):
        return pl.ds(pl.multiple_of(base + rb * sb, sb), sb)

    def in_copy(rb, slot):
        return pltpu.make_async_copy(x_hbm.at[rows(rb, row0)], stage.at[slot], sem_in.at[slot])

    def out_copy(rb):
        return pltpu.make_async_copy(acc.at[rows(rb)], y_hbm.at[rows(rb, row0)], sem_out)

    @pl.when((u == 0) & (j == 0))
    def _():
        acc[...] = jnp.zeros_like(acc)

    @pl.when(active & (j == 0))
    def _():
        in_copy(0, 0).start()

        def load(rb, carry):
            slot = rb % 2

            @pl.when(rb + 1 < nsb)
            def _():
                in_copy(rb + 1, 1 - slot).start()

            in_copy(rb, slot).wait()
            xs[rows(rb), :] = stage[slot].astype(BF16)
            return carry

        lax.fori_loop(0, nsb, load, 0)

    @pl.when(active)
    def _():
        wgb[...] = wg_ref[...].astype(BF16)
        wub[...] = wu_ref[...].astype(BF16)
        wdb[...] = wd_ref[...].astype(BF16)
        first = j == 0

        def ffn(rb, carry):
            xb = xs[rows(rb), :]
            g = jnp.dot(xb, wgb[...], preferred_element_type=F32) + bg_ref[...]
            v = jnp.dot(xb, wub[...], preferred_element_type=F32) + bu_ref[...]
            g = jnp.minimum(g, SWIGLU_LIMIT)
            v = jnp.clip(v, -SWIGLU_LIMIT, SWIGLU_LIMIT)
            a = (g * _sigmoid(SWIGLU_ALPHA * g) * (v + 1.0)).astype(BF16)
            contrib = jnp.dot(a, wdb[...], preferred_element_type=F32)
            base = jnp.where(first, jnp.broadcast_to(bd_ref[...], contrib.shape), acc[rows(rb), :])
            acc[rows(rb), :] = base + contrib
            return carry

        lax.fori_loop(0, nsb, ffn, 0)

    @pl.when(active & (j == n_ff - 1))
    def _():
        lax.fori_loop(0, nsb, lambda rb, c: (out_copy(rb).start(), c)[1], 0)
        lax.fori_loop(0, nsb, lambda rb, c: (out_copy(rb).wait(), c)[1], 0)


def expert_ffn(x_buf, units, layer, w_gu, b_gu, w_down, b_down):
    cap, d = x_buf.shape
    unit_e, unit_row0, unit_nsb, unit_act = units
    n_units = unit_e.shape[0]
    ff = MOE_FF
    n_ff = D_FF // ff
    max_r = MOE_SB * MOE_MAX_SB

    def wspec(shape, imap):
        def index_map(u, j, ue, r0, nsb, act):
            return imap(ue[u], jnp.where(act[u] == 1, j, n_ff - 1))
        return pl.BlockSpec(shape, index_map)

    grid_spec = pltpu.PrefetchScalarGridSpec(
        num_scalar_prefetch=4,
        grid=(n_units, n_ff),
        in_specs=[
            pl.BlockSpec(memory_space=pl.ANY),
            wspec((None, None, d, ff), lambda e, j: (layer, e, 0, j)),
            wspec((None, None, d, ff), lambda e, j: (layer, e, 0, n_ff + j)),
            wspec((None, None, 1, ff), lambda e, j: (layer, e, 0, j)),
            wspec((None, None, 1, ff), lambda e, j: (layer, e, 0, n_ff + j)),
            wspec((None, None, ff, d), lambda e, j: (layer, e, j, 0)),
            wspec((None, None, 1, d), lambda e, j: (layer, e, 0, 0)),
        ],
        out_specs=pl.BlockSpec(memory_space=pl.ANY),
        scratch_shapes=[pltpu.VMEM((max_r, d), BF16), pltpu.VMEM((2, MOE_SB, d), F32), pltpu.VMEM((max_r, d), F32),
                        pltpu.VMEM((d, ff), BF16), pltpu.VMEM((d, ff), BF16), pltpu.VMEM((ff, d), BF16),
                        pltpu.SemaphoreType.DMA((2,)), pltpu.SemaphoreType.DMA(())],
    )
    return pl.pallas_call(
        functools.partial(_expert_kernel, n_ff=n_ff),
        grid_spec=grid_spec,
        out_shape=jax.ShapeDtypeStruct((cap, d), F32),
        input_output_aliases={4: 0},
        compiler_params=_cparams(2), name="expert_ffn",
    )(unit_e, unit_row0, unit_nsb, unit_act, x_buf, w_gu, w_gu, b_gu, b_gu, w_down, b_down)


def routing_metadata(route, cnt, t):
    sb, max_r = MOE_SB, MOE_SB * MOE_MAX_SB
    idx = route[:, :TOP_K]
    rank = route[:, ROUTE_RANK_LANE:ROUTE_RANK_LANE + TOP_K]
    sizes = cnt[0, :N_EXPERTS].astype(jnp.int32)
    padded = (sizes + sb - 1) // sb * sb
    pend = jnp.cumsum(padded)
    pstart = pend - padded
    dest = (pstart[idx] + rank).reshape(-1).astype(jnp.int32)
    cap = t * TOP_K + N_EXPERTS * sb
    units_e = (padded + max_r - 1) // max_r
    uend = jnp.cumsum(units_e)
    ustart = uend - units_e
    n_units = cap // max_r + N_EXPERTS
    u = jnp.arange(n_units, dtype=jnp.int32)
    n_act = uend[-1]
    active = u < n_act
    ue = jnp.minimum(jnp.searchsorted(uend, u, side='right'), N_EXPERTS - 1).astype(jnp.int32)
    ue = jnp.where(active, ue, ue[jnp.maximum(n_act - 1, 0)])
    q = u - ustart[ue]
    row0 = jnp.where(active, pstart[ue] + q * max_r, 0)
    nsb = jnp.where(active, jnp.clip((padded[ue] - q * max_r) // sb, 0, MOE_MAX_SB), 0)
    return dest, cap, (ue, row0.astype(jnp.int32), nsb.astype(jnp.int32), active.astype(jnp.int32))


def _combine_kernel(dest, y_hbm, gate_ref, xc_ref, xl_ref, gf_ref, lng_ref, lnb_ref, o_ref, ybuf, sem,
                    *, alpha, n_ctx_tiles):
    i = pl.program_id(0)
    n = pl.num_programs(0)
    tm = o_ref.shape[0]

    def gather(blk, slot):
        for t in range(tm):
            for k in range(TOP_K):
                row = dest[(blk * tm + t) * TOP_K + k]
                pltpu.make_async_copy(y_hbm.at[pl.ds(row, 1)], ybuf.at[slot, k, pl.ds(t, 1)], sem.at[slot]).start()

    @pl.when(i == 0)
    def _():
        gather(0, 0)

    @pl.when(i + 1 < n)
    def _():
        gather(i + 1, (i + 1) % 2)

    slot = i % 2
    for k in range(TOP_K):
        pltpu.make_async_copy(y_hbm.at[pl.ds(0, tm)], ybuf.at[slot, k], sem.at[slot]).wait()
    gate = gate_ref[...]
    f = gate[:, 0:1] * ybuf[slot, 0]
    for k in range(1, TOP_K):
        f = f + gate[:, k:k + 1] * ybuf[slot, k]
    x = _two_group_rows(i, n_ctx_tiles, xc_ref, xl_ref)
    y = alpha * x + gf_ref[...] * f
    o_ref[...] = _layer_norm_rows(y) * lng_ref[...] + lnb_ref[...]


def combine_block(y_buf, dest, gate, x_ctx, x_lat, n_seq_lat, mods, layer, ln_g, ln_b, alpha):
    rows_ctx, d = x_ctx.shape
    rows = rows_ctx + x_lat.shape[0]
    tm = COMBINE_TM
    n_ctx_tiles, n_lat_tiles = rows_ctx // tm, x_lat.shape[0] // tm
    row_of = _two_group_cond_row(tm, rows_ctx, n_seq_lat)
    vec = pl.BlockSpec((1, d), lambda i, dst: (0, 0))
    grid_spec = pltpu.PrefetchScalarGridSpec(
        num_scalar_prefetch=1, grid=(rows // tm,),
        in_specs=[pl.BlockSpec(memory_space=pl.ANY),
                  pl.BlockSpec((tm, LANE), lambda i, dst: (i, 0))]
                 + _two_group_specs(tm, d, n_ctx_tiles, n_lat_tiles)
                 + [_mod_spec(layer, 5, row_of), vec, vec],
        out_specs=pl.BlockSpec((tm, d), lambda i, dst: (i, 0)),
        scratch_shapes=[pltpu.VMEM((2, TOP_K, tm, d), F32), pltpu.SemaphoreType.DMA((2,))])
    return pl.pallas_call(
        functools.partial(_combine_kernel, alpha=alpha, n_ctx_tiles=n_ctx_tiles), grid_spec=grid_spec,
        out_shape=jax.ShapeDtypeStruct((rows, d), F32),
        compiler_params=_cparams(1), name="combine",
    )(dest, y_buf, gate, x_ctx, x_lat, mods, ln_g, ln_b)


def _rope_tables(n_tokens):
    pos = jnp.arange(n_tokens)
    row = (pos // GRID_W).astype(F32)
    col = (pos % GRID_W).astype(F32)

    def angles(rot_dim):
        n_freq = rot_dim // 4
        inv = 1.0 / (ROPE_BASE ** (jnp.arange(n_freq, dtype=F32) / n_freq))
        ang = jnp.concatenate([row[:, None] * inv, col[:, None] * inv], -1)
        return jnp.cos(ang), jnp.sin(ang)

    c, s = angles(HEAD_DIM)
    rope_ab = (jnp.concatenate([c, c], -1), jnp.concatenate([-s, s], -1))
    c, s = angles(C_ROPE)
    one = jnp.ones((n_tokens, LANE - C_ROPE), F32)
    zero32 = jnp.zeros_like(s)
    zero64 = jnp.zeros((n_tokens, LANE - C_ROPE), F32)
    rope_c = (jnp.concatenate([c, c, one], -1), jnp.concatenate([-s, zero32, zero64], -1),
              jnp.concatenate([zero32, s, zero64], -1))
    return rope_ab, rope_c


def _mixer_block(x, x_row0, layer, lam_init, alpha, mods, cond_row0, n_batch, n_seq, wts, prm, caches, ropes):
    rows = n_batch * n_seq
    latent = caches is not None
    rope_ab, rope_c = ropes if latent else (None, None)

    rpc = n_seq if latent else rows
    z = ln_mod_matmul(x, x_row0, rows, mods, layer, 0, rpc, cond_row0, wts['w_main'],
                      mode="rope" if latent else "plain", rope=rope_ab, n_seq=n_seq)
    gates = ln_mod_matmul(x, x_row0, rows, mods, layer, 0, rpc, cond_row0, wts['w_gates'], mode="sigmoid",
                          n_seq=n_seq)

    tq = min(512, n_seq)
    nqb = n_seq // tq
    zrow = lambda w, col: pl.BlockSpec((n_seq, w), lambda b, h, i: (b, col(h)))
    zq = lambda w, col: pl.BlockSpec((tq, w), lambda b, h, i: (b * nqb + i, col(h)))

    if latent:
        ck = caches['a_k'].reshape(caches['a_k'].shape[:3] + (A_KV_HEADS * HEAD_DIM,))
        cv = caches['a_v'].reshape(ck.shape)
        oa = banded_attention(z, ck, cv, layer, prm['sink'], n_batch, n_seq)
    else:
        oa = attention(
            (n_batch, A_HEADS, nqb),
            [(z, zq(HEAD_DIM, lambda h: COL_QA // HEAD_DIM + h))],
            [[(z, zrow(HEAD_DIM, lambda h: COL_KA // HEAD_DIM + h // A_GROUP))]],
            [(z, zrow(HEAD_DIM, lambda h: COL_VA // HEAD_DIM + h // A_GROUP))],
            zq(HEAD_DIM, lambda h: h), jax.ShapeDtypeStruct((rows, A_HEADS * HEAD_DIM), BF16),
            scale=HEAD_DIM ** -0.5, sink=prm['sink'], name="attn_a_ctx")

    qb0, kb0 = COL_QB // B_HEAD_DIM, COL_KB // B_HEAD_DIM
    k_ops = [[(z, zrow(B_HEAD_DIM, lambda h, m=m: kb0 + m * B_HEADS + h)) for m in range(2)]]
    v_ops = [(z, zrow(B_V_DIM, lambda h: COL_VB // B_V_DIM + h))]
    if latent:
        l_ctx = caches['b_k'].shape[2]
        cbk = caches['b_k'].reshape(caches['b_k'].shape[:3] + (2 * B_HEADS * B_HEAD_DIM,))
        cbv = caches['b_v'].reshape(caches['b_v'].shape[:3] + (B_HEADS * B_V_DIM,))
        cspec = lambda w, col: pl.BlockSpec((None, None, l_ctx, w), lambda b, h, i: (b, layer, 0, col(h)))
        k_ops = [[(cbk, cspec(B_HEAD_DIM, lambda h, m=m: m * B_HEADS + h)) for m in range(2)]] + k_ops
        v_ops = [(cbv, cspec(B_V_DIM, lambda h: h))] + v_ops
    vec3 = lambda a: (a.reshape(a.shape[0], 1, a.shape[1]),
                      pl.BlockSpec((None, 1, a.shape[1]), lambda b, h, i: (layer, 0, 0)))
    ob = attention(
        (n_batch, B_HEADS, nqb),
        [(z, zq(B_HEAD_DIM, lambda h, m=m: qb0 + m * B_HEADS + h)) for m in range(2)],
        k_ops, v_ops, zq(B_V_DIM, lambda h: h), jax.ShapeDtypeStruct((rows, B_HEADS * B_V_DIM), BF16),
        scale=B_HEAD_DIM ** -0.5,
        diff_ops=[vec3(prm['lam_q1']), vec3(prm['lam_k1']), vec3(prm['lam_q2']), vec3(prm['lam_k2']),
                  vec3(prm['subln'])],
        lam_init=lam_init, name="attn_b")

    q_c = c_query(z, prm['q_norm'], wts['w_uq'], rope_c, n_seq)
    tmc = min(512, n_seq if latent else rows)
    per_seq = max(n_seq // tmc, 1)
    kv_out = c_keyvalue(
        (z, pl.BlockSpec((tmc, C_KV_LORA), lambda i: (i, COL_CKV // C_KV_LORA))),
        (z, pl.BlockSpec((tmc, LANE), lambda i: (i, COL_KR // LANE))),
        rows, tmc, (rows // tmc,), prm['kv_norm'], wts['w_uk'], wts['w_uv'],
        rope_c=rope_c, rope_spec=pl.BlockSpec((tmc, LANE), lambda i: (i % per_seq, 0)), emit_norm=not latent)
    k_c, v_c = kv_out[0], kv_out[1]
    crow = lambda w: pl.BlockSpec((n_seq, w), lambda b, h, i: (b, h))
    cq_spec = lambda w: pl.BlockSpec((tq, w), lambda b, h, i: (b * nqb + i, h))
    k_ops = [[(k_c, crow(C_QK_PAD))]]
    v_ops = [(v_c, crow(C_VDIM))]
    if latent:
        l_ctx = caches['c_kv'].shape[2]
        kx, vx = c_keyvalue(
            (caches['c_kv'], pl.BlockSpec((None, None, l_ctx, C_KV_LORA), lambda b, s: (b, layer, 0, 0))),
            (caches['c_kr'], pl.BlockSpec((None, None, l_ctx, LANE), lambda b, s: (b, layer, 0, 0))),
            n_batch * l_ctx, l_ctx, (n_batch, 1), None, wts['w_uk'], wts['w_uv'])
        xrow = lambda w: pl.BlockSpec((l_ctx, w), lambda b, h, i: (b, h))
        k_ops = [[(kx, xrow(C_QK_PAD))]] + k_ops
        v_ops = [(vx, xrow(C_VDIM))] + v_ops
    oc = attention(
        (n_batch, C_HEADS, nqb), [(q_c, cq_spec(C_QK_PAD))], k_ops, v_ops,
        cq_spec(C_VDIM), jax.ShapeDtypeStruct((rows, C_HEADS * C_VDIM), BF16),
        scale=(C_NOPE + C_ROPE) ** -0.5, name="attn_c")

    x = merge_block(oa, ob, oc, gates, x, x_row0, mods, layer, rpc, cond_row0, prm['ln1_g'], prm['ln1_b'],
                    wts['w_br_a'], wts['w_br_b'], wts['w_br_c'], wts['w_out'], alpha)

    state = None
    if not latent:
        state = (z[:, COL_KA:COL_KA + 256], z[:, COL_VA:COL_VA + 256], z[:, COL_KB:COL_KB + 1024],
                 z[:, COL_VB:COL_VB + 1024], kv_out[2], z[:, COL_KR:COL_KR + C_ROPE])
    return x, state


def _moe_block(x_ctx, x_lat, n_seq_lat, layer, alpha, mods, wts, prm):
    rows = x_ctx.shape[0] + x_lat.shape[0]
    h, route, gate, cnt = router(x_ctx, x_lat, n_seq_lat, mods, layer, wts['w_router'], prm['b_router'])
    dest, cap, units = routing_metadata(route, cnt, rows)
    x_buf = dispatch_rows(h, dest, cap)
    y_buf = expert_ffn(x_buf, units, layer, wts['w_gu'], wts['b_gu'], wts['w_down'], wts['b_down'])
    return combine_block(y_buf, dest, gate, x_ctx, x_lat, n_seq_lat, mods, layer, prm['ln2_g'], prm['ln2_b'], alpha)


def kernel(x_prompt, x_sample, c, cache_a_k, cache_a_v, cache_b_k, cache_b_v, cache_c_kv, cache_c_kr, c_ctx, w_ada, b_ada, w_in, sink_a, lam_q1, lam_k1, lam_q2, lam_k2, subln_b, q_norm_c, kv_norm_c, w_uq, w_ukv, w_br_a, w_br_b, w_br_c, w_out, ln1_g, ln1_b, ln2_g, ln2_b, w_router, b_router, w_gu, b_gu, w_down, b_down):
    depth = w_in.shape[0]
    batch, seq, d = x_prompt.shape
    dec_batch, dec_seq, _ = x_sample.shape
    alpha = (2 * depth) ** 0.25
    assert d == D_MODEL and dec_batch + 1 <= 8

    cond8 = jnp.zeros((8, d), F32).at[0].set(c_ctx).at[1:1 + dec_batch].set(c)
    mods = adaln_all(cond8, w_ada, b_ada).reshape(depth, 8, 6, 1, d)

    ropes = _rope_tables(dec_seq)
    n_exp = w_gu.shape[1]
    b_gu4 = b_gu.reshape(depth, n_exp, 1, 2 * D_FF)
    b_down4 = b_down.reshape(depth, n_exp, 1, d)
    caches = dict(a_k=cache_a_k, a_v=cache_a_v, b_k=cache_b_k, b_v=cache_b_v, c_kv=cache_c_kv,
                  c_kr=jnp.pad(cache_c_kr, ((0, 0), (0, 0), (0, 0), (0, LANE - C_ROPE))))

    rows_ctx = batch * seq
    x_all = jnp.concatenate([x_prompt.reshape(rows_ctx, d), x_sample.reshape(dec_batch * dec_seq, d)], axis=0)
    states = []
    for l in range(depth):
        lam_init = 0.8 - 0.6 * math.exp(-0.3 * l)
        wl = w_in[l]
        w_main = jnp.concatenate([wl[:, :N_IN_SRC], jnp.zeros((d, N_MAIN - N_IN_SRC), F32)], axis=1).astype(BF16)
        w_uq_l = w_uq[l].reshape(C_Q_LORA, C_HEADS, C_NOPE + C_ROPE)
        w_uq_p = jnp.pad(w_uq_l, ((0, 0), (0, 0), (0, C_QK_PAD - C_NOPE - C_ROPE))).reshape(C_Q_LORA, -1).astype(BF16)
        w_ukv_l = w_ukv[l].reshape(C_KV_LORA, C_HEADS, C_NOPE + C_VDIM)
        wts = dict(
            w_main=w_main, w_gates=wl[:, N_IN_SRC:].astype(BF16), w_uq=w_uq_p,
            w_uk=w_ukv_l[:, :, :C_NOPE].reshape(C_KV_LORA, -1).astype(BF16),
            w_uv=w_ukv_l[:, :, C_NOPE:].reshape(C_KV_LORA, -1).astype(BF16),
            w_br_a=w_br_a[l].astype(BF16), w_br_b=w_br_b[l].astype(BF16), w_br_c=w_br_c[l].astype(BF16),
            w_out=w_out[l].astype(BF16),
            w_router=jnp.pad(w_router[l], ((0, 0), (0, LANE - n_exp))),
            w_gu=w_gu, b_gu=b_gu4, w_down=w_down, b_down=b_down4)
        prm = dict(sink=sink_a[l], lam_q1=lam_q1, lam_k1=lam_k1, lam_q2=lam_q2, lam_k2=lam_k2, subln=subln_b,
                   q_norm=q_norm_c[l][None], kv_norm=kv_norm_c[l][None],
                   ln1_g=ln1_g[l][None], ln1_b=ln1_b[l][None], ln2_g=ln2_g[l][None], ln2_b=ln2_b[l][None],
                   b_router=jnp.pad(b_router[l], (0, LANE - n_exp))[None])
        xp, st = _mixer_block(x_all, 0, l, lam_init, alpha, mods, 0, batch, seq, wts, prm, None, None)
        xs, _ = _mixer_block(x_all, rows_ctx, l, lam_init, alpha, mods, 1, dec_batch, dec_seq, wts, prm, caches, ropes)
        states.append(st)
        x_all = _moe_block(xp, xs, dec_seq, l, alpha, mods, wts, prm)

    def stack(i, shape):
        return jnp.stack([s[i].reshape((batch, seq) + shape) for s in states], axis=1)

    return (x_all[:rows_ctx].reshape(batch, seq, d), x_all[rows_ctx:].reshape(dec_batch, dec_seq, d),
            stack(0, (A_KV_HEADS, HEAD_DIM)), stack(1, (A_KV_HEADS, HEAD_DIM)),
            stack(2, (2, B_HEADS, B_HEAD_DIM)), stack(3, (B_HEADS, B_V_DIM)),
            stack(4, (C_KV_LORA,)), stack(5, (C_ROPE,)))
```

```python
import functools
import math

import jax
import jax.numpy as jnp
from jax import lax
from jax.experimental import pallas as pl
from jax.experimental.pallas import tpu as pltpu

F32 = jnp.float32
BF16 = jnp.bfloat16

D_MODEL = 2048
GRID_W = 64
WINDOW = 128
A_HEADS = 8
A_KV_HEADS = 2
A_GROUP = A_HEADS // A_KV_HEADS
HEAD_DIM = 128
B_HEADS = 4
B_HEAD_DIM = 128
B_V_DIM = 2 * B_HEAD_DIM
C_HEADS = 8
C_NOPE = 128
C_ROPE = 64
C_VDIM = 128
C_Q_LORA = 512
C_KV_LORA = 256
C_QK_PAD = 256
N_BRANCH = 3
N_EXPERTS = 32
TOP_K = 4
D_FF = 2048
SWIGLU_LIMIT = 7.0
SWIGLU_ALPHA = 1.702
ROPE_BASE = 10000.0
LN_EPS = 1e-5
RMS_EPS = 1e-6
NEG_INF = -1e30

COL_QA = 0
COL_KA = 1024
COL_VA = 1280
COL_QB = 1536
COL_KB = 2560
COL_VB = 3584
COL_CQ = 4608
COL_CKV = 5120
COL_KR = 5376
N_MAIN = 5632
N_GATES = N_BRANCH * D_MODEL
N_IN_SRC = COL_KR + C_ROPE

LANE = 128
V7X_VMEM_LIMIT = 56 * 1024 * 1024

LOG2_E = 1.4426950408889634

PROJ_TM = 1024
PROJ_TN = 512
ATTN_SUB = 256
MERGE_TM = 256
ROUTER_TM = 256
MOE_SB = 256
MOE_MAX_SB = 8
MOE_FF = 256
DISPATCH_TM = 256
COMBINE_TM = 128


def _cparams(n_grid):
    return pltpu.CompilerParams(dimension_semantics=("arbitrary",) * n_grid, vmem_limit_bytes=V7X_VMEM_LIMIT)


def _sigmoid(x):
    return 1.0 / (1.0 + jnp.exp(-x))


def _layer_norm_rows(x):
    mu = jnp.mean(x, axis=-1, keepdims=True)
    xc = x - mu
    var = jnp.mean(xc * xc, axis=-1, keepdims=True)
    return xc * lax.rsqrt(var + LN_EPS)


def _rms_rows(x, g):
    return x * lax.rsqrt(jnp.mean(x * x, axis=-1, keepdims=True) + RMS_EPS) * g


def _rope_half128(x, cosf, sinf):
    return x * cosf + pltpu.roll(x, 64, 1) * sinf


def _rope_pad64(x, cf, sa, sb):
    return x * cf + pltpu.roll(x, 96, 1) * sa + pltpu.roll(x, 32, 1) * sb


def _adaln_kernel(c_ref, w_ref, b_ref, o_ref):
    c = c_ref[...]
    s = (c * _sigmoid(c)).astype(BF16)
    o_ref[...] = jnp.dot(s, w_ref[...].astype(BF16), preferred_element_type=F32) + b_ref[...]


def adaln_all(cond8, w_ada, b_ada):
    depth, d, n = w_ada.shape
    tn = 1024
    return pl.pallas_call(
        _adaln_kernel,
        grid=(depth, n // tn),
        in_specs=[
            pl.BlockSpec((8, d), lambda l, j: (0, 0)),
            pl.BlockSpec((None, d, tn), lambda l, j: (l, 0, j)),
            pl.BlockSpec((None, 1, tn), lambda l, j: (l, 0, j)),
        ],
        out_specs=pl.BlockSpec((None, 8, tn), lambda l, j: (l, 0, j)),
        out_shape=jax.ShapeDtypeStruct((depth, 8, n), F32),
        compiler_params=_cparams(2),
        name="adaln",
    )(cond8, w_ada, b_ada.reshape(depth, 1, n))


def _mod_spec(layer, slot, row_of_tile):
    return pl.BlockSpec((None, None, None, 1, D_MODEL), lambda *g: (layer, row_of_tile(*g), slot, 0, 0))


ROPE_FULL_TILES = (0, 1, 3, 4, 5, 6)
ROPE_HALF_TILE = 2


def _lnmm_kernel(*refs, mode):
    if mode == "rope":
        x_ref, sc_ref, sh_ref, w_ref, cos_ref, sin_ref, o_ref, h_scr = refs
    else:
        x_ref, sc_ref, sh_ref, w_ref, o_ref, h_scr = refs
    j = pl.program_id(1)

    @pl.when(j == 0)
    def _():
        h = _layer_norm_rows(x_ref[...]) * (1.0 + sc_ref[...]) + sh_ref[...]
        h_scr[...] = h.astype(BF16)

    acc = jnp.dot(h_scr[...], w_ref[...], preferred_element_type=F32)
    if mode == "sigmoid":
        o_ref[...] = _sigmoid(acc)
    elif mode == "plain":
        o_ref[...] = acc
    else:
        is_full = functools.reduce(jnp.logical_or, [j == t for t in ROPE_FULL_TILES])
        is_half = j == ROPE_HALF_TILE

        def rotated(n_heads):
            cosf, sinf = cos_ref[...], sin_ref[...]
            for hh in range(PROJ_TN // HEAD_DIM):
                blk = acc[:, hh * HEAD_DIM:(hh + 1) * HEAD_DIM]
                if hh < n_heads:
                    blk = _rope_half128(blk, cosf, sinf)
                o_ref[:, hh * HEAD_DIM:(hh + 1) * HEAD_DIM] = blk

        @pl.when(is_full)
        def _():
            rotated(PROJ_TN // HEAD_DIM)

        @pl.when(is_half)
        def _():
            rotated(A_KV_HEADS)

        @pl.when(jnp.logical_not(jnp.logical_or(is_full, is_half)))
        def _():
            o_ref[...] = acc


def ln_mod_matmul(x, x_row0, rows, mods, layer, slot, rows_per_cond, cond_row0, w, *, mode, rope=None, n_seq=None):
    d = x.shape[1]
    n = w.shape[1]
    tm = min(PROJ_TM, n_seq if mode == "rope" else rows_per_cond)
    tn = PROJ_TN
    off = x_row0 // tm
    assert x_row0 % tm == 0 and rows % tm == 0
    row_of = lambda i, j: cond_row0 + (i * tm) // rows_per_cond
    in_specs = [
        pl.BlockSpec((tm, d), lambda i, j: (i + off, 0)),
        _mod_spec(layer, slot + 1, row_of),
        _mod_spec(layer, slot, row_of),
        pl.BlockSpec((d, tn), lambda i, j: (0, j)),
    ]
    args = [x, mods, mods, w]
    if mode == "rope":
        per_seq = n_seq // tm
        in_specs += [pl.BlockSpec((tm, LANE), lambda i, j: (i % per_seq, 0))] * 2
        args += list(rope)
    return pl.pallas_call(
        functools.partial(_lnmm_kernel, mode=mode),
        grid=(rows // tm, n // tn),
        in_specs=in_specs,
        out_specs=pl.BlockSpec((tm, tn), lambda i, j: (i, j)),
        out_shape=jax.ShapeDtypeStruct((rows, n), F32),
        scratch_shapes=[pltpu.VMEM((tm, d), BF16)],
        compiler_params=_cparams(2),
        name="ln_mod_matmul_" + mode,
    )(*args)


def _attn_kernel(*refs, n_maps, n_seg, scale, use_sink, diff, lam_init):
    pos = 0
    if use_sink:
        sink_ref = refs[0]
        pos = 1
    q_refs = refs[pos:pos + n_maps]
    pos += n_maps
    k_refs = [refs[pos + sg * n_maps: pos + (sg + 1) * n_maps] for sg in range(n_seg)]
    pos += n_seg * n_maps
    v_refs = refs[pos:pos + n_seg]
    pos += n_seg
    if diff:
        lq1, lk1, lq2, lk2, subln_ref = refs[pos:pos + 5]
        pos += 5
    o_ref = refs[pos]

    c2 = scale * LOG2_E
    ks = [[k_refs[sg][m][...].astype(BF16) for m in range(n_maps)] for sg in range(n_seg)]
    vs = [v_refs[sg][...].astype(BF16) for sg in range(n_seg)]
    if use_sink:
        sink_raw = sink_ref[pl.program_id(1)] * (1.0 / scale)
    if diff:
        f = jnp.sum(lq1[...] * lk1[...], axis=-1, keepdims=True)
        g = jnp.sum(lq2[...] * lk2[...], axis=-1, keepdims=True)
        lam = jnp.exp(f) - jnp.exp(g) + lam_init
    tq = o_ref.shape[0]
    sub = min(tq, ATTN_SUB)
    for r0 in range(0, tq, sub):
        es, dens = [], []
        for m in range(n_maps):
            q = q_refs[m][r0:r0 + sub, :].astype(BF16)
            s = [lax.dot_general(q, ks[sg][m], (((1,), (1,)), ((), ())), preferred_element_type=F32)
                 for sg in range(n_seg)]
            mx = functools.reduce(jnp.maximum, [jnp.max(t, axis=-1, keepdims=True) for t in s])
            if use_sink:
                mx = jnp.maximum(mx, sink_raw)
            e = [jnp.exp2((t - mx) * c2) for t in s]
            den = functools.reduce(jnp.add, [jnp.sum(t, axis=-1, keepdims=True) for t in e])
            if use_sink:
                den = den + jnp.exp2((sink_raw - mx) * c2)
            es.append(e)
            dens.append(den)
        if diff:
            coef = lam * dens[0] * (1.0 / dens[1])
            a = [es[0][sg] - coef * es[1][sg] for sg in range(n_seg)]
        else:
            a = es[0]
        o = functools.reduce(jnp.add, [jnp.dot(a[sg].astype(BF16), vs[sg], preferred_element_type=F32)
                                       for sg in range(n_seg)])
        o = o * (1.0 / dens[0])
        if diff:
            o = _rms_rows(o, subln_ref[...]) * (1.0 - lam_init)
        o_ref[r0:r0 + sub, :] = o.astype(o_ref.dtype)


def attention(grid, q_ops, k_ops, v_ops, out_spec, out_shape, *, scale, sink=None, diff_ops=None, lam_init=0.0,
              name="attn"):
    n_maps, n_seg = len(q_ops), len(k_ops)
    ops = []
    if sink is not None:
        ops.append((sink, pl.BlockSpec(memory_space=pltpu.SMEM)))
    ops += list(q_ops)
    for seg in k_ops:
        ops += list(seg)
    ops += list(v_ops)
    if diff_ops is not None:
        ops += list(diff_ops)
    kern = functools.partial(_attn_kernel, n_maps=n_maps, n_seg=n_seg, scale=scale, use_sink=sink is not None,
                             diff=diff_ops is not None, lam_init=lam_init)
    return pl.pallas_call(
        kern, grid=grid,
        in_specs=[s for _, s in ops],
        out_specs=out_spec, out_shape=out_shape,
        compiler_params=_cparams(len(grid)), name=name,
    )(*[a for a, _ in ops])


def _band_kernel(sink_ref, q_ref, kp_ref, kc_ref, kn_ref, kx_ref, vp_ref, vc_ref, vn_ref, vx_ref, o_ref, *, n_blocks):
    g = pl.program_id(1)
    i = pl.program_id(2)
    w = WINDOW
    q = q_ref[...]
    qs = jnp.concatenate([q[:, r * HEAD_DIM:(r + 1) * HEAD_DIM] for r in range(A_GROUP)], axis=0).astype(BF16)
    kcat = jnp.concatenate([kp_ref[...], kc_ref[...], kn_ref[...], kx_ref[...]], axis=0).astype(BF16)
    vcat = jnp.concatenate([vp_ref[...], vc_ref[...], vn_ref[...], vx_ref[...]], axis=0).astype(BF16)
    s = lax.dot_general(qs, kcat, (((1,), (1,)), ((), ())), preferred_element_type=F32) * (HEAD_DIM ** -0.5)
    n_keys = s.shape[1]
    row = lax.broadcasted_iota(jnp.int32, s.shape, 0) & (w - 1)
    col = lax.broadcasted_iota(jnp.int32, s.shape, 1)
    prev_slack = jnp.where(i > 0, col - row, -1)
    next_slack = jnp.where(i < n_blocks - 1, row - col + 2 * w, -1)
    slack = jnp.where(col < w, prev_slack, jnp.where(col < 2 * w, 0, jnp.where(col < 3 * w, next_slack, 0)))
    s = jnp.where(slack >= 0, s, NEG_INF)
    sink = jnp.concatenate([jnp.full((w, 1), sink_ref[g * A_GROUP + r], F32) for r in range(A_GROUP)], axis=0)
    mx = jnp.maximum(jnp.max(s, axis=-1, keepdims=True), sink)
    e = jnp.exp(s - mx)
    den = jnp.sum(e, axis=-1, keepdims=True) + jnp.exp(sink - mx)
    p = (e * (1.0 / den)).astype(BF16)
    o = jnp.dot(p, vcat, preferred_element_type=F32)
    for r in range(A_GROUP):
        o_ref[:, r * HEAD_DIM:(r + 1) * HEAD_DIM] = o[r * w:(r + 1) * w].astype(o_ref.dtype)


def banded_attention(z, cache_k, cache_v, layer, sink, n_batch, n_seq):
    nb = n_seq // WINDOW
    l_ctx = cache_k.shape[2]
    qw = A_GROUP * HEAD_DIM
    kcol, vcol = COL_KA // HEAD_DIM, COL_VA // HEAD_DIM

    def blk(col0, shift):
        def imap(b, g, i):
            return (b * nb + jnp.clip(i + shift, 0, nb - 1), col0 + g)
        return pl.BlockSpec((WINDOW, HEAD_DIM), imap)

    ctx_spec = pl.BlockSpec((None, None, l_ctx, HEAD_DIM), lambda b, g, i: (b, layer, 0, g))
    return pl.pallas_call(
        functools.partial(_band_kernel, n_blocks=nb),
        grid=(n_batch, A_KV_HEADS, nb),
        in_specs=[pl.BlockSpec(memory_space=pltpu.SMEM),
                  pl.BlockSpec((WINDOW, qw), lambda b, g, i: (b * nb + i, g)),
                  blk(kcol, -1), blk(kcol, 0), blk(kcol, 1), ctx_spec,
                  blk(vcol, -1), blk(vcol, 0), blk(vcol, 1), ctx_spec],
        out_specs=pl.BlockSpec((WINDOW, qw), lambda b, g, i: (b * nb + i, g)),
        out_shape=jax.ShapeDtypeStruct((n_batch * n_seq, A_HEADS * HEAD_DIM), BF16),
        compiler_params=_cparams(3), name="banded_attn",
    )(sink, z, z, z, z, cache_k, z, z, z, cache_v)


def _cq_kernel(*refs, rope):
    if rope:
        cq_ref, g_ref, w_ref, cf_ref, sa_ref, sb_ref, o_ref = refs
    else:
        cq_ref, g_ref, w_ref, o_ref = refs
    y = _rms_rows(cq_ref[...], g_ref[...]).astype(BF16)
    q = jnp.dot(y, w_ref[...], preferred_element_type=F32)
    if rope:
        cf, sa, sb = cf_ref[...], sa_ref[...], sb_ref[...]
    for h in range(C_HEADS):
        lo = h * C_QK_PAD
        o_ref[:, lo:lo + C_NOPE] = q[:, lo:lo + C_NOPE].astype(o_ref.dtype)
        blk = q[:, lo + C_NOPE:lo + C_QK_PAD]
        if rope:
            blk = _rope_pad64(blk, cf, sa, sb)
        o_ref[:, lo + C_NOPE:lo + C_QK_PAD] = blk.astype(o_ref.dtype)


def c_query(z, q_norm, w_uq_p, rope_c, n_seq):
    rows = z.shape[0]
    rope = rope_c is not None
    tm = min(512, n_seq if rope else rows)
    in_specs = [pl.BlockSpec((tm, C_Q_LORA), lambda i: (i, COL_CQ // C_Q_LORA)),
                pl.BlockSpec((1, C_Q_LORA), lambda i: (0, 0)),
                pl.BlockSpec(w_uq_p.shape, lambda i: (0, 0))]
    args = [z, q_norm, w_uq_p]
    if rope:
        per_seq = n_seq // tm
        in_specs += [pl.BlockSpec((tm, LANE), lambda i: (i % per_seq, 0))] * 3
        args += list(rope_c)
    return pl.pallas_call(
        functools.partial(_cq_kernel, rope=rope),
        grid=(rows // tm,), in_specs=in_specs,
        out_specs=pl.BlockSpec((tm, C_HEADS * C_QK_PAD), lambda i: (i, 0)),
        out_shape=jax.ShapeDtypeStruct((rows, C_HEADS * C_QK_PAD), BF16),
        compiler_params=_cparams(1), name="c_query",
    )(*args)


def _ckv_kernel(*refs, normalize, rope, emit_norm):
    refs = list(refs)
    ckv_ref, kr_ref = refs[:2]
    pos = 2
    if normalize:
        g_ref = refs[pos]
        pos += 1
    wn_ref, wv_ref = refs[pos:pos + 2]
    pos += 2
    if rope:
        cf_ref, sa_ref, sb_ref = refs[pos:pos + 3]
        pos += 3
    k_ref, v_ref = refs[pos:pos + 2]
    pos += 2
    ckv = ckv_ref[...]
    if normalize:
        ckv = _rms_rows(ckv, g_ref[...])
    if emit_norm:
        refs[pos][...] = ckv
    cb = ckv.astype(BF16)
    kn = jnp.dot(cb, wn_ref[...], preferred_element_type=F32)
    v_ref[...] = jnp.dot(cb, wv_ref[...], preferred_element_type=F32).astype(v_ref.dtype)
    kr = kr_ref[...]
    if rope:
        kr = _rope_pad64(kr, cf_ref[...], sa_ref[...], sb_ref[...])
    kr = kr.astype(k_ref.dtype)
    for h in range(C_HEADS):
        lo = h * C_QK_PAD
        k_ref[:, lo:lo + C_NOPE] = kn[:, h * C_NOPE:(h + 1) * C_NOPE].astype(k_ref.dtype)
        k_ref[:, lo + C_NOPE:lo + C_QK_PAD] = kr


def c_keyvalue(ckv_op, kr_op, rows, tm, grid, kv_norm, w_nope, w_v, rope_c=None, rope_spec=None, emit_norm=False):
    ops = [ckv_op, kr_op]
    full = lambda a: (a, pl.BlockSpec(a.shape, lambda *g: (0,) * a.ndim))
    if kv_norm is not None:
        ops.append(full(kv_norm))
    ops += [full(w_nope), full(w_v)]
    if rope_c is not None:
        ops += [(t, rope_spec) for t in rope_c]
    row_map = lambda *g: (functools.reduce(lambda a, b: a * grid[1] + b, g) if len(g) > 1 else g[0], 0)
    out_specs = [pl.BlockSpec((tm, C_HEADS * C_QK_PAD), row_map), pl.BlockSpec((tm, C_HEADS * C_VDIM), row_map)]
    out_shape = [jax.ShapeDtypeStruct((rows, C_HEADS * C_QK_PAD), BF16),
                 jax.ShapeDtypeStruct((rows, C_HEADS * C_VDIM), BF16)]
    if emit_norm:
        out_specs.append(pl.BlockSpec((tm, C_KV_LORA), row_map))
        out_shape.append(jax.ShapeDtypeStruct((rows, C_KV_LORA), F32))
    return pl.pallas_call(
        functools.partial(_ckv_kernel, normalize=kv_norm is not None, rope=rope_c is not None, emit_norm=emit_norm),
        grid=grid, in_specs=[s for _, s in ops], out_specs=out_specs, out_shape=out_shape,
        compiler_params=_cparams(len(grid)), name="c_keyvalue",
    )(*[a for a, _ in ops])


def _merge_kernel(oa_ref, ob_ref, oc_ref, g_ref, x_ref, ga_ref, lng_ref, lnb_ref, wa_ref, wb_ref, wc_ref, wo_ref,
                  o_ref, mix_scr, *, alpha):
    d = D_MODEL
    chunk = 512
    for c in range(d // chunk):
        sl = slice(c * chunk, (c + 1) * chunk)
        ya = jnp.dot(oa_ref[...], wa_ref[:, sl], preferred_element_type=F32)
        yb = jnp.dot(ob_ref[...], wb_ref[:, sl], preferred_element_type=F32)
        yc = jnp.dot(oc_ref[...], wc_ref[:, sl], preferred_element_type=F32)
        mixed = (g_ref[:, c * chunk:(c + 1) * chunk] * ya
                 + g_ref[:, d + c * chunk:d + (c + 1) * chunk] * yb
                 + g_ref[:, 2 * d + c * chunk:2 * d + (c + 1) * chunk] * yc)
        mix_scr[:, sl] = mixed.astype(BF16)
    out = jnp.dot(mix_scr[...], wo_ref[...], preferred_element_type=F32)
    y = alpha * x_ref[...] + ga_ref[...] * out
    o_ref[...] = _layer_norm_rows(y) * lng_ref[...] + lnb_ref[...]


def merge_block(oa, ob, oc, gates, x, x_row0, mods, layer, rows_per_cond, cond_row0, ln_g, ln_b, wa, wb, wc, wo,
                alpha):
    rows, d = oa.shape[0], x.shape[1]
    tm = MERGE_TM
    off = x_row0 // tm
    assert x_row0 % tm == 0
    row_of = lambda i: cond_row0 + (i * tm) // rows_per_cond
    const = lambda a: pl.BlockSpec(a.shape, lambda i: (0, 0), pipeline_mode=pl.Buffered(1))
    rowblk = lambda a: pl.BlockSpec((tm, a.shape[1]), lambda i: (i, 0))
    vec = pl.BlockSpec((1, d), lambda i: (0, 0))
    return pl.pallas_call(
        functools.partial(_merge_kernel, alpha=alpha),
        grid=(rows // tm,),
        in_specs=[rowblk(oa), rowblk(ob), rowblk(oc), rowblk(gates),
                  pl.BlockSpec((tm, d), lambda i: (i + off, 0)), _mod_spec(layer, 2, row_of),
                  vec, vec, const(wa), const(wb), const(wc), const(wo)],
        out_specs=pl.BlockSpec((tm, d), lambda i: (i, 0)),
        out_shape=jax.ShapeDtypeStruct((rows, d), F32),
        scratch_shapes=[pltpu.VMEM((tm, d), BF16)],
        compiler_params=_cparams(1), name="merge",
    )(oa, ob, oc, gates, x, mods, ln_g, ln_b, wa, wb, wc, wo)


ROUTE_RANK_LANE = TOP_K


def _two_group_rows(i, n_ctx_tiles, xc_ref, xl_ref):
    return jnp.where(i < n_ctx_tiles, xc_ref[...], xl_ref[...])


def _two_group_specs(tm, d, n_ctx_tiles, n_lat_tiles):
    return [pl.BlockSpec((tm, d), lambda i, *_: (jnp.minimum(i, n_ctx_tiles - 1), 0)),
            pl.BlockSpec((tm, d), lambda i, *_: (jnp.clip(i - n_ctx_tiles, 0, n_lat_tiles - 1), 0))]


def _two_group_cond_row(tm, rows_ctx, n_seq_lat):
    return lambda i, *_: jnp.where(i * tm < rows_ctx, 0, 1 + (i * tm - rows_ctx) // n_seq_lat)


def _router_kernel(xc_ref, xl_ref, sc_ref, sh_ref, wr_ref, br_ref, h_ref, route_ref, gate_ref, cnt_ref, carry_scr,
                   *, n_ctx_tiles):
    i = pl.program_id(0)

    @pl.when(i == 0)
    def _():
        carry_scr[...] = jnp.zeros_like(carry_scr)

    x = _two_group_rows(i, n_ctx_tiles, xc_ref, xl_ref)
    h = _layer_norm_rows(x) * (1.0 + sc_ref[...]) + sh_ref[...]
    h_ref[...] = h
    logits = jnp.dot(h, wr_ref[...], preferred_element_type=F32, precision=lax.Precision.HIGHEST) + br_ref[...]
    tm = logits.shape[0]
    lane = lax.broadcasted_iota(jnp.int32, (tm, LANE), 1)
    cur = jnp.where(lane < N_EXPERTS, logits, -jnp.inf)
    vals, idxs = [], []
    for _ in range(TOP_K):
        m = jnp.max(cur, axis=-1, keepdims=True)
        ik = jnp.min(jnp.where(cur == m, lane, LANE), axis=-1, keepdims=True)
        vals.append(m)
        idxs.append(ik)
        cur = jnp.where(lane == ik, -jnp.inf, cur)
    e = [jnp.exp(v - vals[0]) for v in vals]
    inv = 1.0 / functools.reduce(jnp.add, e)
    onehot = functools.reduce(jnp.add, [(lane == ik).astype(F32) for ik in idxs])
    r_i = lax.broadcasted_iota(jnp.int32, (tm, tm), 0)
    c_i = lax.broadcasted_iota(jnp.int32, (tm, tm), 1)
    tri = (c_i < r_i).astype(BF16)
    rank_full = jnp.dot(tri, onehot.astype(BF16), preferred_element_type=F32) + carry_scr[...]
    route = jnp.zeros((tm, LANE), jnp.int32)
    gate = jnp.zeros((tm, LANE), F32)
    for k in range(TOP_K):
        rk = jnp.sum(jnp.where(lane == idxs[k], rank_full, 0.0), axis=-1, keepdims=True).astype(jnp.int32)
        route = jnp.where(lane == k, idxs[k], route)
        route = jnp.where(lane == ROUTE_RANK_LANE + k, rk, route)
        gate = jnp.where(lane == k, e[k] * inv, gate)
    route_ref[...] = route
    gate_ref[...] = gate
    carry_scr[...] = carry_scr[...] + jnp.sum(onehot, axis=0, keepdims=True)
    cnt_ref[...] = carry_scr[...]


def router(x_ctx, x_lat, n_seq_lat, mods, layer, w_router_p, b_router_p):
    rows_ctx, d = x_ctx.shape
    rows = rows_ctx + x_lat.shape[0]
    tm = ROUTER_TM
    n_ctx_tiles, n_lat_tiles = rows_ctx // tm, x_lat.shape[0] // tm
    row_of = _two_group_cond_row(tm, rows_ctx, n_seq_lat)
    rowblk = lambda w: pl.BlockSpec((tm, w), lambda i: (i, 0))
    return pl.pallas_call(
        functools.partial(_router_kernel, n_ctx_tiles=n_ctx_tiles), grid=(rows // tm,),
        in_specs=_two_group_specs(tm, d, n_ctx_tiles, n_lat_tiles) + [
            _mod_spec(layer, 4, row_of), _mod_spec(layer, 3, row_of),
            pl.BlockSpec((d, LANE), lambda i: (0, 0)), pl.BlockSpec((1, LANE), lambda i: (0, 0))],
        out_specs=[rowblk(d), rowblk(LANE), rowblk(LANE), pl.BlockSpec((1, LANE), lambda i: (0, 0))],
        out_shape=[jax.ShapeDtypeStruct((rows, d), F32), jax.ShapeDtypeStruct((rows, LANE), jnp.int32),
                   jax.ShapeDtypeStruct((rows, LANE), F32), jax.ShapeDtypeStruct((1, LANE), F32)],
        scratch_shapes=[pltpu.VMEM((1, LANE), F32)],
        compiler_params=_cparams(1), name="router",
    )(x_ctx, x_lat, mods, mods, w_router_p, b_router_p)


def _dispatch_kernel(dest, h_ref, init_hbm, x_hbm, sem):
    del init_hbm
    i = pl.program_id(0)
    tm = h_ref.shape[0]
    for t in range(tm):
        for k in range(TOP_K):
            row = dest[(i * tm + t) * TOP_K + k]
            pltpu.make_async_copy(h_ref.at[pl.ds(t, 1)], x_hbm.at[pl.ds(row, 1)], sem).start()
    for k in range(TOP_K):
        pltpu.make_async_copy(h_ref, x_hbm.at[pl.ds(0, tm)], sem).wait()


def dispatch_rows(h, dest, cap):
    t, d = h.shape
    tm = DISPATCH_TM
    grid_spec = pltpu.PrefetchScalarGridSpec(
        num_scalar_prefetch=1, grid=(t // tm,),
        in_specs=[pl.BlockSpec((tm, d), lambda i, dst: (i, 0)), pl.BlockSpec(memory_space=pl.ANY)],
        out_specs=pl.BlockSpec(memory_space=pl.ANY),
        scratch_shapes=[pltpu.SemaphoreType.DMA(())])
    return pl.pallas_call(
        _dispatch_kernel, grid_spec=grid_spec,
        out_shape=jax.ShapeDtypeStruct((cap, d), F32),
        input_output_aliases={2: 0},
        compiler_params=_cparams(1), name="dispatch",
    )(dest, h, jnp.zeros((cap, d), F32))


def _expert_kernel(unit_e, unit_row0, unit_nsb, unit_act,
                   x_hbm, wg_ref, wu_ref, bg_ref, bu_ref, wd_ref, bd_ref, y_hbm,
                   xs, stage, acc, wgb, wub, wdb, sem_in, sem_out, *, n_ff):
    del unit_e
    u = pl.program_id(0)
    j = pl.program_id(1)
    sb = MOE_SB
    nsb = unit_nsb[u]
    row0 = unit_row0[u]
    active = unit_act[u] == 1

    def rows(rb, base=0):
        return pl.ds(pl.multiple_of(base + rb * sb, sb), sb)

    def in_copy(rb, slot):
        return pltpu.make_async_copy(x_hbm.at[rows(rb, row0)], stage.at[slot], sem_in.at[slot])

    def out_copy(rb):
        return pltpu.make_async_copy(acc.at[rows(rb)], y_hbm.at[rows(rb, row0)], sem_out)

    @pl.when((u == 0) & (j == 0))
    def _():
        acc[...] = jnp.zeros_like(acc)

    first = j == 0
    last = j == n_ff - 1

    @pl.when(active & first)
    def _():
        in_copy(0, 0).start()

    @pl.when(active)
    def _():
        wgb[...] = wg_ref[...].astype(BF16)
        wub[...] = wu_ref[...].astype(BF16)
        wdb[...] = wd_ref[...].astype(BF16)

        def ffn(rb, carry):
            @pl.when(first)
            def _():
                slot = rb % 2

                @pl.when(rb + 1 < nsb)
                def _():
                    in_copy(rb + 1, 1 - slot).start()

                in_copy(rb, slot).wait()
                xs[rows(rb), :] = stage[slot].astype(BF16)

            xb = xs[rows(rb), :]
            g = jnp.dot(xb, wgb[...], preferred_element_type=F32) + bg_ref[...]
            v = jnp.dot(xb, wub[...], preferred_element_type=F32) + bu_ref[...]
            g = jnp.minimum(g, SWIGLU_LIMIT)
            v = jnp.clip(v, -SWIGLU_LIMIT, SWIGLU_LIMIT)
            a = (g * _sigmoid(SWIGLU_ALPHA * g) * (v + 1.0)).astype(BF16)
            contrib = jnp.dot(a, wdb[...], preferred_element_type=F32)
            base = jnp.where(first, jnp.broadcast_to(bd_ref[...], contrib.shape), acc[rows(rb), :])
            acc[rows(rb), :] = base + contrib

            @pl.when(last)
            def _():
                out_copy(rb).start()

            return carry

        lax.fori_loop(0, nsb, ffn, 0)

    @pl.when(active & last)
    def _():
        lax.fori_loop(0, nsb, lambda rb, c: (out_copy(rb).wait(), c)[1], 0)


def expert_ffn(x_buf, units, layer, w_gu, b_gu, w_down, b_down):
    cap, d = x_buf.shape
    unit_e, unit_row0, unit_nsb, unit_act = units
    n_units = unit_e.shape[0]
    ff = MOE_FF
    n_ff = D_FF // ff
    max_r = MOE_SB * MOE_MAX_SB

    def wspec(shape, imap):
        def index_map(u, j, ue, r0, nsb, act):
            return imap(ue[u], jnp.where(act[u] == 1, j, n_ff - 1))
        return pl.BlockSpec(shape, index_map)

    grid_spec = pltpu.PrefetchScalarGridSpec(
        num_scalar_prefetch=4,
        grid=(n_units, n_ff),
        in_specs=[
            pl.BlockSpec(memory_space=pl.ANY),
            wspec((None, None, d, ff), lambda e, j: (layer, e, 0, j)),
            wspec((None, None, d, ff), lambda e, j: (layer, e, 0, n_ff + j)),
            wspec((None, None, 1, ff), lambda e, j: (layer, e, 0, j)),
            wspec((None, None, 1, ff), lambda e, j: (layer, e, 0, n_ff + j)),
            wspec((None, None, ff, d), lambda e, j: (layer, e, j, 0)),
            wspec((None, None, 1, d), lambda e, j: (layer, e, 0, 0)),
        ],
        out_specs=pl.BlockSpec(memory_space=pl.ANY),
        scratch_shapes=[pltpu.VMEM((max_r, d), BF16), pltpu.VMEM((2, MOE_SB, d), F32), pltpu.VMEM((max_r, d), F32),
                        pltpu.VMEM((d, ff), BF16), pltpu.VMEM((d, ff), BF16), pltpu.VMEM((ff, d), BF16),
                        pltpu.SemaphoreType.DMA((2,)), pltpu.SemaphoreType.DMA(())],
    )
    return pl.pallas_call(
        functools.partial(_expert_kernel, n_ff=n_ff),
        grid_spec=grid_spec,
        out_shape=jax.ShapeDtypeStruct((cap, d), F32),
        input_output_aliases={4: 0},
        compiler_params=_cparams(2), name="expert_ffn",
    )(unit_e, unit_row0, unit_nsb, unit_act, x_buf, w_gu, w_gu, b_gu, b_gu, w_down, b_down)


def routing_metadata(route, cnt, t):
    sb, max_r = MOE_SB, MOE_SB * MOE_MAX_SB
    idx = route[:, :TOP_K]
    rank = route[:, ROUTE_RANK_LANE:ROUTE_RANK_LANE + TOP_K]
    sizes = cnt[0, :N_EXPERTS].astype(jnp.int32)
    padded = (sizes + sb - 1) // sb * sb
    pend = jnp.cumsum(padded)
    pstart = pend - padded
    dest = (pstart[idx] + rank).reshape(-1).astype(jnp.int32)
    cap = t * TOP_K + N_EXPERTS * sb
    units_e = (padded + max_r - 1) // max_r
    uend = jnp.cumsum(units_e)
    ustart = uend - units_e
    n_units = cap // max_r + N_EXPERTS
    u = jnp.arange(n_units, dtype=jnp.int32)
    n_act = uend[-1]
    active = u < n_act
    ue = jnp.minimum(jnp.searchsorted(uend, u, side='right'), N_EXPERTS - 1).astype(jnp.int32)
    ue = jnp.where(active, ue, ue[jnp.maximum(n_act - 1, 0)])
    q = u - ustart[ue]
    row0 = jnp.where(active, pstart[ue] + q * max_r, 0)
    nsb = jnp.where(active, jnp.clip((padded[ue] - q * max_r) // sb, 0, MOE_MAX_SB), 0)
    return dest, cap, (ue, row0.astype(jnp.int32), nsb.astype(jnp.int32), active.astype(jnp.int32))


def _combine_kernel(dest, y_hbm, gate_ref, xc_ref, xl_ref, gf_ref, lng_ref, lnb_ref, o_ref, ybuf, sem,
                    *, alpha, n_ctx_tiles):
    i = pl.program_id(0)
    n = pl.num_programs(0)
    tm = o_ref.shape[0]

    def gather(blk, slot):
        for t in range(tm):
            for k in range(TOP_K):
                row = dest[(blk * tm + t) * TOP_K + k]
                pltpu.make_async_copy(y_hbm.at[pl.ds(row, 1)], ybuf.at[slot, k, pl.ds(t, 1)], sem.at[slot]).start()

    @pl.when(i == 0)
    def _():
        gather(0, 0)

    @pl.when(i + 1 < n)
    def _():
        gather(i + 1, (i + 1) % 2)

    slot = i % 2
    for k in range(TOP_K):
        pltpu.make_async_copy(y_hbm.at[pl.ds(0, tm)], ybuf.at[slot, k], sem.at[slot]).wait()
    gate = gate_ref[...]
    f = gate[:, 0:1] * ybuf[slot, 0]
    for k in range(1, TOP_K):
        f = f + gate[:, k:k + 1] * ybuf[slot, k]
    x = _two_group_rows(i, n_ctx_tiles, xc_ref, xl_ref)
    y = alpha * x + gf_ref[...] * f
    o_ref[...] = _layer_norm_rows(y) * lng_ref[...] + lnb_ref[...]


def combine_block(y_buf, dest, gate, x_ctx, x_lat, n_seq_lat, mods, layer, ln_g, ln_b, alpha):
    rows_ctx, d = x_ctx.shape
    rows = rows_ctx + x_lat.shape[0]
    tm = COMBINE_TM
    n_ctx_tiles, n_lat_tiles = rows_ctx // tm, x_lat.shape[0] // tm
    row_of = _two_group_cond_row(tm, rows_ctx, n_seq_lat)
    vec = pl.BlockSpec((1, d), lambda i, dst: (0, 0))
    grid_spec = pltpu.PrefetchScalarGridSpec(
        num_scalar_prefetch=1, grid=(rows // tm,),
        in_specs=[pl.BlockSpec(memory_space=pl.ANY),
                  pl.BlockSpec((tm, LANE), lambda i, dst: (i, 0))]
                 + _two_group_specs(tm, d, n_ctx_tiles, n_lat_tiles)
                 + [_mod_spec(layer, 5, row_of), vec, vec],
        out_specs=pl.BlockSpec((tm, d), lambda i, dst: (i, 0)),
        scratch_shapes=[pltpu.VMEM((2, TOP_K, tm, d), F32), pltpu.SemaphoreType.DMA((2,))])
    return pl.pallas_call(
        functools.partial(_combine_kernel, alpha=alpha, n_ctx_tiles=n_ctx_tiles), grid_spec=grid_spec,
        out_shape=jax.ShapeDtypeStruct((rows, d), F32),
        compiler_params=_cparams(1), name="combine",
    )(dest, y_buf, gate, x_ctx, x_lat, mods, ln_g, ln_b)


def _rope_tables(n_tokens):
    pos = jnp.arange(n_tokens)
    row = (pos // GRID_W).astype(F32)
    col = (pos % GRID_W).astype(F32)

    def angles(rot_dim):
        n_freq = rot_dim // 4
        inv = 1.0 / (ROPE_BASE ** (jnp.arange(n_freq, dtype=F32) / n_freq))
        ang = jnp.concatenate([row[:, None] * inv, col[:, None] * inv], -1)
        return jnp.cos(ang), jnp.sin(ang)

    c, s = angles(HEAD_DIM)
    rope_ab = (jnp.concatenate([c, c], -1), jnp.concatenate([-s, s], -1))
    c, s = angles(C_ROPE)
    one = jnp.ones((n_tokens, LANE - C_ROPE), F32)
    zero32 = jnp.zeros_like(s)
    zero64 = jnp.zeros((n_tokens, LANE - C_ROPE), F32)
    rope_c = (jnp.concatenate([c, c, one], -1), jnp.concatenate([-s, zero32, zero64], -1),
              jnp.concatenate([zero32, s, zero64], -1))
    return rope_ab, rope_c


def _mixer_block(x, x_row0, layer, lam_init, alpha, mods, cond_row0, n_batch, n_seq, wts, prm, caches, ropes):
    rows = n_batch * n_seq
    latent = caches is not None
    rope_ab, rope_c = ropes if latent else (None, None)

    rpc = n_seq if latent else rows
    z = ln_mod_matmul(x, x_row0, rows, mods, layer, 0, rpc, cond_row0, wts['w_main'],
                      mode="rope" if latent else "plain", rope=rope_ab, n_seq=n_seq)
    gates = ln_mod_matmul(x, x_row0, rows, mods, layer, 0, rpc, cond_row0, wts['w_gates'], mode="sigmoid",
                          n_seq=n_seq)

    tq = min(512, n_seq)
    nqb = n_seq // tq
    zrow = lambda w, col: pl.BlockSpec((n_seq, w), lambda b, h, i: (b, col(h)))
    zq = lambda w, col: pl.BlockSpec((tq, w), lambda b, h, i: (b * nqb + i, col(h)))

    if latent:
        ck = caches['a_k'].reshape(caches['a_k'].shape[:3] + (A_KV_HEADS * HEAD_DIM,))
        cv = caches['a_v'].reshape(ck.shape)
        oa = banded_attention(z, ck, cv, layer, prm['sink'], n_batch, n_seq)
    else:
        oa = attention(
            (n_batch, A_HEADS, nqb),
            [(z, zq(HEAD_DIM, lambda h: COL_QA // HEAD_DIM + h))],
            [[(z, zrow(HEAD_DIM, lambda h: COL_KA // HEAD_DIM + h // A_GROUP))]],
            [(z, zrow(HEAD_DIM, lambda h: COL_VA // HEAD_DIM + h // A_GROUP))],
            zq(HEAD_DIM, lambda h: h), jax.ShapeDtypeStruct((rows, A_HEADS * HEAD_DIM), BF16),
            scale=HEAD_DIM ** -0.5, sink=prm['sink'], name="attn_a_ctx")

    qb0, kb0 = COL_QB // B_HEAD_DIM, COL_KB // B_HEAD_DIM
    k_ops = [[(z, zrow(B_HEAD_DIM, lambda h, m=m: kb0 + m * B_HEADS + h)) for m in range(2)]]
    v_ops = [(z, zrow(B_V_DIM, lambda h: COL_VB // B_V_DIM + h))]
    if latent:
        l_ctx = caches['b_k'].shape[2]
        cbk = caches['b_k'].reshape(caches['b_k'].shape[:3] + (2 * B_HEADS * B_HEAD_DIM,))
        cbv = caches['b_v'].reshape(caches['b_v'].shape[:3] + (B_HEADS * B_V_DIM,))
        cspec = lambda w, col: pl.BlockSpec((None, None, l_ctx, w), lambda b, h, i: (b, layer, 0, col(h)))
        k_ops = [[(cbk, cspec(B_HEAD_DIM, lambda h, m=m: m * B_HEADS + h)) for m in range(2)]] + k_ops
        v_ops = [(cbv, cspec(B_V_DIM, lambda h: h))] + v_ops
    vec3 = lambda a: (a.reshape(a.shape[0], 1, a.shape[1]),
                      pl.BlockSpec((None, 1, a.shape[1]), lambda b, h, i: (layer, 0, 0)))
    ob = attention(
        (n_batch, B_HEADS, nqb),
        [(z, zq(B_HEAD_DIM, lambda h, m=m: qb0 + m * B_HEADS + h)) for m in range(2)],
        k_ops, v_ops, zq(B_V_DIM, lambda h: h), jax.ShapeDtypeStruct((rows, B_HEADS * B_V_DIM), BF16),
        scale=B_HEAD_DIM ** -0.5,
        diff_ops=[vec3(prm['lam_q1']), vec3(prm['lam_k1']), vec3(prm['lam_q2']), vec3(prm['lam_k2']),
                  vec3(prm['subln'])],
        lam_init=lam_init, name="attn_b")

    q_c = c_query(z, prm['q_norm'], wts['w_uq'], rope_c, n_seq)
    tmc = min(512, n_seq if latent else rows)
    per_seq = max(n_seq // tmc, 1)
    kv_out = c_keyvalue(
        (z, pl.BlockSpec((tmc, C_KV_LORA), lambda i: (i, COL_CKV // C_KV_LORA))),
        (z, pl.BlockSpec((tmc, LANE), lambda i: (i, COL_KR // LANE))),
        rows, tmc, (rows // tmc,), prm['kv_norm'], wts['w_uk'], wts['w_uv'],
        rope_c=rope_c, rope_spec=pl.BlockSpec((tmc, LANE), lambda i: (i % per_seq, 0)), emit_norm=not latent)
    k_c, v_c = kv_out[0], kv_out[1]
    crow = lambda w: pl.BlockSpec((n_seq, w), lambda b, h, i: (b, h))
    cq_spec = lambda w: pl.BlockSpec((tq, w), lambda b, h, i: (b * nqb + i, h))
    k_ops = [[(k_c, crow(C_QK_PAD))]]
    v_ops = [(v_c, crow(C_VDIM))]
    if latent:
        l_ctx = caches['c_kv'].shape[2]
        kx, vx = c_keyvalue(
            (caches['c_kv'], pl.BlockSpec((None, None, l_ctx, C_KV_LORA), lambda b, s: (b, layer, 0, 0))),
            (caches['c_kr'], pl.BlockSpec((None, None, l_ctx, LANE), lambda b, s: (b, layer, 0, 0))),
            n_batch * l_ctx, l_ctx, (n_batch, 1), None, wts['w_uk'], wts['w_uv'])
        xrow = lambda w: pl.BlockSpec((l_ctx, w), lambda b, h, i: (b, h))
        k_ops = [[(kx, xrow(C_QK_PAD))]] + k_ops
        v_ops = [(vx, xrow(C_VDIM))] + v_ops
    oc = attention(
        (n_batch, C_HEADS, nqb), [(q_c, cq_spec(C_QK_PAD))], k_ops, v_ops,
        cq_spec(C_VDIM), jax.ShapeDtypeStruct((rows, C_HEADS * C_VDIM), BF16),
        scale=(C_NOPE + C_ROPE) ** -0.5, name="attn_c")

    x = merge_block(oa, ob, oc, gates, x, x_row0, mods, layer, rpc, cond_row0, prm['ln1_g'], prm['ln1_b'],
                    wts['w_br_a'], wts['w_br_b'], wts['w_br_c'], wts['w_out'], alpha)

    state = None
    if not latent:
        state = (z[:, COL_KA:COL_KA + 256], z[:, COL_VA:COL_VA + 256], z[:, COL_KB:COL_KB + 1024],
                 z[:, COL_VB:COL_VB + 1024], kv_out[2], z[:, COL_KR:COL_KR + C_ROPE])
    return x, state


def _moe_block(x_ctx, x_lat, n_seq_lat, layer, alpha, mods, wts, prm):
    rows = x_ctx.shape[0] + x_lat.shape[0]
    h, route, gate, cnt = router(x_ctx, x_lat, n_seq_lat, mods, layer, wts['w_router'], prm['b_router'])
    dest, cap, units = routing_metadata(route, cnt, rows)
    x_buf = dispatch_rows(h, dest, cap)
    y_buf = expert_ffn(x_buf, units, layer, wts['w_gu'], wts['b_gu'], wts['w_down'], wts['b_down'])
    return combine_block(y_buf, dest, gate, x_ctx, x_lat, n_seq_lat, mods, layer, prm['ln2_g'], prm['ln2_b'], alpha)


def kernel(x_prompt, x_sample, c, cache_a_k, cache_a_v, cache_b_k, cache_b_v, cache_c_kv, cache_c_kr, c_ctx, w_ada, b_ada, w_in, sink_a, lam_q1, lam_k1, lam_q2, lam_k2, subln_b, q_norm_c, kv_norm_c, w_uq, w_ukv, w_br_a, w_br_b, w_br_c, w_out, ln1_g, ln1_b, ln2_g, ln2_b, w_router, b_router, w_gu, b_gu, w_down, b_down):
    depth = w_in.shape[0]
    batch, seq, d = x_prompt.shape
    dec_batch, dec_seq, _ = x_sample.shape
    alpha = (2 * depth) ** 0.25
    assert d == D_MODEL and dec_batch + 1 <= 8

    cond8 = jnp.zeros((8, d), F32).at[0].set(c_ctx).at[1:1 + dec_batch].set(c)
    mods = adaln_all(cond8, w_ada, b_ada).reshape(depth, 8, 6, 1, d)

    ropes = _rope_tables(dec_seq)
    n_exp = w_gu.shape[1]
    b_gu4 = b_gu.reshape(depth, n_exp, 1, 2 * D_FF)
    b_down4 = b_down.reshape(depth, n_exp, 1, d)
    caches = dict(a_k=cache_a_k, a_v=cache_a_v, b_k=cache_b_k, b_v=cache_b_v, c_kv=cache_c_kv,
                  c_kr=jnp.pad(cache_c_kr, ((0, 0), (0, 0), (0, 0), (0, LANE - C_ROPE))))

    rows_ctx = batch * seq
    x_all = jnp.concatenate([x_prompt.reshape(rows_ctx, d), x_sample.reshape(dec_batch * dec_seq, d)], axis=0)
    states = []
    for l in range(depth):
        lam_init = 0.8 - 0.6 * math.exp(-0.3 * l)
        wl = w_in[l]
        w_main = jnp.concatenate([wl[:, :N_IN_SRC], jnp.zeros((d, N_MAIN - N_IN_SRC), F32)], axis=1).astype(BF16)
        w_uq_l = w_uq[l].reshape(C_Q_LORA, C_HEADS, C_NOPE + C_ROPE)
        w_uq_p = jnp.pad(w_uq_l, ((0, 0), (0, 0), (0, C_QK_PAD - C_NOPE - C_ROPE))).reshape(C_Q_LORA, -1).astype(BF16)
        w_ukv_l = w_ukv[l].reshape(C_KV_LORA, C_HEADS, C_NOPE + C_VDIM)
        wts = dict(
            w_main=w_main, w_gates=wl[:, N_IN_SRC:].astype(BF16), w_uq=w_uq_p,
            w_uk=w_ukv_l[:, :, :C_NOPE].reshape(C_KV_LORA, -1).astype(BF16),
            w_uv=w_ukv_l[:, :, C_NOPE:].reshape(C_KV_LORA, -1).astype(BF16),
            w_br_a=w_br_a[l].astype(BF16), w_br_b=w_br_b[l].astype(BF16), w_br_c=w_br_c[l].astype(BF16),
            w_out=w_out[l].astype(BF16),
            w_router=jnp.pad(w_router[l], ((0, 0), (0, LANE - n_exp))),
            w_gu=w_gu, b_gu=b_gu4, w_down=w_down, b_down=b_down4)
        prm = dict(sink=sink_a[l], lam_q1=lam_q1, lam_k1=lam_k1, lam_q2=lam_q2, lam_k2=lam_k2, subln=subln_b,
                   q_norm=q_norm_c[l][None], kv_norm=kv_norm_c[l][None],
                   ln1_g=ln1_g[l][None], ln1_b=ln1_b[l][None], ln2_g=ln2_g[l][None], ln2_b=ln2_b[l][None],
                   b_router=jnp.pad(b_router[l], (0, LANE - n_exp))[None])
        xp, st = _mixer_block(x_all, 0, l, lam_init, alpha, mods, 0, batch, seq, wts, prm, None, None)
        xs, _ = _mixer_block(x_all, rows_ctx, l, lam_init, alpha, mods, 1, dec_batch, dec_seq, wts, prm, caches, ropes)
        states.append(st)
        x_all = _moe_block(xp, xs, dec_seq, l, alpha, mods, wts, prm)

    def stack(i, shape):
        return jnp.stack([s[i].reshape((batch, seq) + shape) for s in states], axis=1)

    return (x_all[:rows_ctx].reshape(batch, seq, d), x_all[rows_ctx:].reshape(dec_batch, dec_seq, d),
            stack(0, (A_KV_HEADS, HEAD_DIM)), stack(1, (A_KV_HEADS, HEAD_DIM)),
            stack(2, (2, B_HEADS, B_HEAD_DIM)), stack(3, (B_HEADS, B_V_DIM)),
            stack(4, (C_KV_LORA,)), stack(5, (C_ROPE,)))
```

```python
import functools
import math

import jax
import jax.numpy as jnp
from jax import lax
from jax.experimental import pallas as pl
from jax.experimental.pallas import tpu as pltpu

F32 = jnp.float32
BF16 = jnp.bfloat16

D_MODEL = 2048
GRID_W = 64
WINDOW = 128
A_HEADS = 8
A_KV_HEADS = 2
A_GROUP = A_HEADS // A_KV_HEADS
HEAD_DIM = 128
B_HEADS = 4
B_HEAD_DIM = 128
B_V_DIM = 2 * B_HEAD_DIM
C_HEADS = 8
C_NOPE = 128
C_ROPE = 64
C_VDIM = 128
C_Q_LORA = 512
C_KV_LORA = 256
C_QK_PAD = 256
N_BRANCH = 3
N_EXPERTS = 32
TOP_K = 4
D_FF = 2048
SWIGLU_LIMIT = 7.0
SWIGLU_ALPHA = 1.702
ROPE_BASE = 10000.0
LN_EPS = 1e-5
RMS_EPS = 1e-6
NEG_INF = -1e30

COL_QA = 0
COL_KA = 1024
COL_VA = 1280
COL_QB = 1536
COL_KB = 2560
COL_VB = 3584
COL_CQ = 4608
COL_CKV = 5120
COL_KR = 5376
N_MAIN = 5632
N_GATES = N_BRANCH * D_MODEL
N_IN_SRC = COL_KR + C_ROPE

LANE = 128
V7X_VMEM_LIMIT = 56 * 1024 * 1024

LOG2_E = 1.4426950408889634

PROJ_TM = 1024
PROJ_TN = 512
ATTN_SUB = 256
MERGE_TM = 256
ROUTER_TM = 256
MOE_SB = 256
MOE_MAX_SB = 8
MOE_FF = 256
DISPATCH_TM = 256
COMBINE_TM = 128


def _cparams(n_grid):
    return pltpu.CompilerParams(dimension_semantics=("arbitrary",) * n_grid, vmem_limit_bytes=V7X_VMEM_LIMIT)


def _sigmoid(x):
    return 1.0 / (1.0 + jnp.exp(-x))


def _layer_norm_rows(x):
    mu = jnp.mean(x, axis=-1, keepdims=True)
    xc = x - mu
    var = jnp.mean(xc * xc, axis=-1, keepdims=True)
    return xc * lax.rsqrt(var + LN_EPS)


def _rms_rows(x, g):
    return x * lax.rsqrt(jnp.mean(x * x, axis=-1, keepdims=True) + RMS_EPS) * g


def _rope_half128(x, cosf, sinf):
    return x * cosf + pltpu.roll(x, 64, 1) * sinf


def _rope_pad64(x, cf, sa, sb):
    return x * cf + pltpu.roll(x, 96, 1) * sa + pltpu.roll(x, 32, 1) * sb


def _adaln_kernel(c_ref, w_ref, b_ref, o_ref):
    c = c_ref[...]
    s = (c * _sigmoid(c)).astype(BF16)
    o_ref[...] = jnp.dot(s, w_ref[...].astype(BF16), preferred_element_type=F32) + b_ref[...]


def adaln_all(cond8, w_ada, b_ada):
    depth, d, n = w_ada.shape
    tn = 1024
    return pl.pallas_call(
        _adaln_kernel,
        grid=(depth, n // tn),
        in_specs=[
            pl.BlockSpec((8, d), lambda l, j: (0, 0)),
            pl.BlockSpec((None, d, tn), lambda l, j: (l, 0, j)),
            pl.BlockSpec((None, 1, tn), lambda l, j: (l, 0, j)),
        ],
        out_specs=pl.BlockSpec((None, 8, tn), lambda l, j: (l, 0, j)),
        out_shape=jax.ShapeDtypeStruct((depth, 8, n), F32),
        compiler_params=_cparams(2),
        name="adaln",
    )(cond8, w_ada, b_ada.reshape(depth, 1, n))


def _mod_spec(layer, slot, row_of_tile):
    return pl.BlockSpec((None, None, None, 1, D_MODEL), lambda *g: (layer, row_of_tile(*g), slot, 0, 0))


ROPE_FULL_TILES = (0, 1, 3, 4, 5, 6)
ROPE_HALF_TILE = 2


def _lnmm_kernel(*refs, mode):
    if mode == "rope":
        x_ref, sc_ref, sh_ref, w_ref, cos_ref, sin_ref, o_ref, h_scr = refs
    else:
        x_ref, sc_ref, sh_ref, w_ref, o_ref, h_scr = refs
    j = pl.program_id(1)

    @pl.when(j == 0)
    def _():
        h = _layer_norm_rows(x_ref[...]) * (1.0 + sc_ref[...]) + sh_ref[...]
        h_scr[...] = h.astype(BF16)

    acc = jnp.dot(h_scr[...], w_ref[...], preferred_element_type=F32)
    if mode == "sigmoid":
        o_ref[...] = _sigmoid(acc)
    elif mode == "plain":
        o_ref[...] = acc
    else:
        is_full = functools.reduce(jnp.logical_or, [j == t for t in ROPE_FULL_TILES])
        is_half = j == ROPE_HALF_TILE

        def rotated(n_heads):
            cosf, sinf = cos_ref[...], sin_ref[...]
            for hh in range(PROJ_TN // HEAD_DIM):
                blk = acc[:, hh * HEAD_DIM:(hh + 1) * HEAD_DIM]
                if hh < n_heads:
                    blk = _rope_half128(blk, cosf, sinf)
                o_ref[:, hh * HEAD_DIM:(hh + 1) * HEAD_DIM] = blk

        @pl.when(is_full)
        def _():
            rotated(PROJ_TN // HEAD_DIM)

        @pl.when(is_half)
        def _():
            rotated(A_KV_HEADS)

        @pl.when(jnp.logical_not(jnp.logical_or(is_full, is_half)))
        def _():
            o_ref[...] = acc


def ln_mod_matmul(x, x_row0, rows, mods, layer, slot, rows_per_cond, cond_row0, w, *, mode, rope=None, n_seq=None):
    d = x.shape[1]
    n = w.shape[1]
    tm = min(PROJ_TM, n_seq if mode == "rope" else rows_per_cond)
    tn = PROJ_TN
    off = x_row0 // tm
    assert x_row0 % tm == 0 and rows % tm == 0
    row_of = lambda i, j: cond_row0 + (i * tm) // rows_per_cond
    in_specs = [
        pl.BlockSpec((tm, d), lambda i, j: (i + off, 0)),
        _mod_spec(layer, slot + 1, row_of),
        _mod_spec(layer, slot, row_of),
        pl.BlockSpec((d, tn), lambda i, j: (0, j)),
    ]
    args = [x, mods, mods, w]
    if mode == "rope":
        per_seq = n_seq // tm
        in_specs += [pl.BlockSpec((tm, LANE), lambda i, j: (i % per_seq, 0))] * 2
        args += list(rope)
    return pl.pallas_call(
        functools.partial(_lnmm_kernel, mode=mode),
        grid=(rows // tm, n // tn),
        in_specs=in_specs,
        out_specs=pl.BlockSpec((tm, tn), lambda i, j: (i, j)),
        out_shape=jax.ShapeDtypeStruct((rows, n), F32),
        scratch_shapes=[pltpu.VMEM((tm, d), BF16)],
        compiler_params=_cparams(2),
        name="ln_mod_matmul_" + mode,
    )(*args)


def _attn_kernel(*refs, n_maps, n_seg, scale, use_sink, diff, lam_init):
    pos = 0
    if use_sink:
        sink_ref = refs[0]
        pos = 1
    q_refs = refs[pos:pos + n_maps]
    pos += n_maps
    k_refs = [refs[pos + sg * n_maps: pos + (sg + 1) * n_maps] for sg in range(n_seg)]
    pos += n_seg * n_maps
    v_refs = refs[pos:pos + n_seg]
    pos += n_seg
    if diff:
        lq1, lk1, lq2, lk2, subln_ref = refs[pos:pos + 5]
        pos += 5
    o_ref = refs[pos]

    c2 = scale * LOG2_E
    ks = [[k_refs[sg][m][...].astype(BF16) for m in range(n_maps)] for sg in range(n_seg)]
    vs = [v_refs[sg][...].astype(BF16) for sg in range(n_seg)]
    if use_sink:
        sink_raw = sink_ref[pl.program_id(1)] * (1.0 / scale)
    if diff:
        f = jnp.sum(lq1[...] * lk1[...], axis=-1, keepdims=True)
        g = jnp.sum(lq2[...] * lk2[...], axis=-1, keepdims=True)
        lam = jnp.exp(f) - jnp.exp(g) + lam_init
    tq = o_ref.shape[0]
    sub = min(tq, ATTN_SUB)
    for r0 in range(0, tq, sub):
        es, dens = [], []
        for m in range(n_maps):
            q = q_refs[m][r0:r0 + sub, :].astype(BF16)
            s = [lax.dot_general(q, ks[sg][m], (((1,), (1,)), ((), ())), preferred_element_type=F32)
                 for sg in range(n_seg)]
            mx = functools.reduce(jnp.maximum, [jnp.max(t, axis=-1, keepdims=True) for t in s])
            if use_sink:
                mx = jnp.maximum(mx, sink_raw)
            e = [jnp.exp2((t - mx) * c2) for t in s]
            den = functools.reduce(jnp.add, [jnp.sum(t, axis=-1, keepdims=True) for t in e])
            if use_sink:
                den = den + jnp.exp2((sink_raw - mx) * c2)
            es.append(e)
            dens.append(den)
        if diff:
            coef = lam * dens[0] * (1.0 / dens[1])
            a = [es[0][sg] - coef * es[1][sg] for sg in range(n_seg)]
        else:
            a = es[0]
        o = functools.reduce(jnp.add, [jnp.dot(a[sg].astype(BF16), vs[sg], preferred_element_type=F32)
                                       for sg in range(n_seg)])
        o = o * (1.0 / dens[0])
        if diff:
            o = _rms_rows(o, subln_ref[...]) * (1.0 - lam_init)
        o_ref[r0:r0 + sub, :] = o.astype(o_ref.dtype)


def attention(grid, q_ops, k_ops, v_ops, out_spec, out_shape, *, scale, sink=None, diff_ops=None, lam_init=0.0,
              name="attn"):
    n_maps, n_seg = len(q_ops), len(k_ops)
    ops = []
    if sink is not None:
        ops.append((sink, pl.BlockSpec(memory_space=pltpu.SMEM)))
    ops += list(q_ops)
    for seg in k_ops:
        ops += list(seg)
    ops += list(v_ops)
    if diff_ops is not None:
        ops += list(diff_ops)
    kern = functools.partial(_attn_kernel, n_maps=n_maps, n_seg=n_seg, scale=scale, use_sink=sink is not None,
                             diff=diff_ops is not None, lam_init=lam_init)
    return pl.pallas_call(
        kern, grid=grid,
        in_specs=[s for _, s in ops],
        out_specs=out_spec, out_shape=out_shape,
        compiler_params=_cparams(len(grid)), name=name,
    )(*[a for a, _ in ops])


def _band_kernel(sink_ref, q_ref, kp_ref, kc_ref, kn_ref, kx_ref, vp_ref, vc_ref, vn_ref, vx_ref, o_ref, *, n_blocks):
    g = pl.program_id(1)
    i = pl.program_id(2)
    w = WINDOW
    q = q_ref[...]
    qs = jnp.concatenate([q[:, r * HEAD_DIM:(r + 1) * HEAD_DIM] for r in range(A_GROUP)], axis=0).astype(BF16)
    kcat = jnp.concatenate([kp_ref[...], kc_ref[...], kn_ref[...], kx_ref[...]], axis=0).astype(BF16)
    vcat = jnp.concatenate([vp_ref[...], vc_ref[...], vn_ref[...], vx_ref[...]], axis=0).astype(BF16)
    s = lax.dot_general(qs, kcat, (((1,), (1,)), ((), ())), preferred_element_type=F32) * (HEAD_DIM ** -0.5)
    n_keys = s.shape[1]
    row = lax.broadcasted_iota(jnp.int32, s.shape, 0) & (w - 1)
    col = lax.broadcasted_iota(jnp.int32, s.shape, 1)
    prev_slack = jnp.where(i > 0, col - row, -1)
    next_slack = jnp.where(i < n_blocks - 1, row - col + 2 * w, -1)
    slack = jnp.where(col < w, prev_slack, jnp.where(col < 2 * w, 0, jnp.where(col < 3 * w, next_slack, 0)))
    s = jnp.where(slack >= 0, s, NEG_INF)
    sink = jnp.concatenate([jnp.full((w, 1), sink_ref[g * A_GROUP + r], F32) for r in range(A_GROUP)], axis=0)
    mx = jnp.maximum(jnp.max(s, axis=-1, keepdims=True), sink)
    e = jnp.exp(s - mx)
    den = jnp.sum(e, axis=-1, keepdims=True) + jnp.exp(sink - mx)
    p = (e * (1.0 / den)).astype(BF16)
    o = jnp.dot(p, vcat, preferred_element_type=F32)
    for r in range(A_GROUP):
        o_ref[:, r * HEAD_DIM:(r + 1) * HEAD_DIM] = o[r * w:(r + 1) * w].astype(o_ref.dtype)


def banded_attention(z, cache_k, cache_v, layer, sink, n_batch, n_seq):
    nb = n_seq // WINDOW
    l_ctx = cache_k.shape[2]
    qw = A_GROUP * HEAD_DIM
    kcol, vcol = COL_KA // HEAD_DIM, COL_VA // HEAD_DIM

    def blk(col0, shift):
        def imap(b, g, i):
            return (b * nb + jnp.clip(i + shift, 0, nb - 1), col0 + g)
        return pl.BlockSpec((WINDOW, HEAD_DIM), imap)

    ctx_spec = pl.BlockSpec((None, None, l_ctx, HEAD_DIM), lambda b, g, i: (b, layer, 0, g))
    return pl.pallas_call(
        functools.partial(_band_kernel, n_blocks=nb),
        grid=(n_batch, A_KV_HEADS, nb),
        in_specs=[pl.BlockSpec(memory_space=pltpu.SMEM),
                  pl.BlockSpec((WINDOW, qw), lambda b, g, i: (b * nb + i, g)),
                  blk(kcol, -1), blk(kcol, 0), blk(kcol, 1), ctx_spec,
                  blk(vcol, -1), blk(vcol, 0), blk(vcol, 1), ctx_spec],
        out_specs=pl.BlockSpec((WINDOW, qw), lambda b, g, i: (b * nb + i, g)),
        out_shape=jax.ShapeDtypeStruct((n_batch * n_seq, A_HEADS * HEAD_DIM), BF16),
        compiler_params=_cparams(3), name="banded_attn",
    )(sink, z, z, z, z, cache_k, z, z, z, cache_v)


def _cq_kernel(*refs, rope):
    if rope:
        cq_ref, g_ref, w_ref, cf_ref, sa_ref, sb_ref, o_ref = refs
    else:
        cq_ref, g_ref, w_ref, o_ref = refs
    y = _rms_rows(cq_ref[...], g_ref[...]).astype(BF16)
    q = jnp.dot(y, w_ref[...], preferred_element_type=F32)
    if rope:
        cf, sa, sb = cf_ref[...], sa_ref[...], sb_ref[...]
    for h in range(C_HEADS):
        lo = h * C_QK_PAD
        o_ref[:, lo:lo + C_NOPE] = q[:, lo:lo + C_NOPE].astype(o_ref.dtype)
        blk = q[:, lo + C_NOPE:lo + C_QK_PAD]
        if rope:
            blk = _rope_pad64(blk, cf, sa, sb)
        o_ref[:, lo + C_NOPE:lo + C_QK_PAD] = blk.astype(o_ref.dtype)


def c_query(z, q_norm, w_uq_p, rope_c, n_seq):
    rows = z.shape[0]
    rope = rope_c is not None
    tm = min(512, n_seq if rope else rows)
    in_specs = [pl.BlockSpec((tm, C_Q_LORA), lambda i: (i, COL_CQ // C_Q_LORA)),
                pl.BlockSpec((1, C_Q_LORA), lambda i: (0, 0)),
                pl.BlockSpec(w_uq_p.shape, lambda i: (0, 0))]
    args = [z, q_norm, w_uq_p]
    if rope:
        per_seq = n_seq // tm
        in_specs += [pl.BlockSpec((tm, LANE), lambda i: (i % per_seq, 0))] * 3
        args += list(rope_c)
    return pl.pallas_call(
        functools.partial(_cq_kernel, rope=rope),
        grid=(rows // tm,), in_specs=in_specs,
        out_specs=pl.BlockSpec((tm, C_HEADS * C_QK_PAD), lambda i: (i, 0)),
        out_shape=jax.ShapeDtypeStruct((rows, C_HEADS * C_QK_PAD), BF16),
        compiler_params=_cparams(1), name="c_query",
    )(*args)


def _ckv_kernel(*refs, normalize, rope, emit_norm):
    refs = list(refs)
    ckv_ref, kr_ref = refs[:2]
    pos = 2
    if normalize:
        g_ref = refs[pos]
        pos += 1
    wn_ref, wv_ref = refs[pos:pos + 2]
    pos += 2
    if rope:
        cf_ref, sa_ref, sb_ref = refs[pos:pos + 3]
        pos += 3
    k_ref, v_ref = refs[pos:pos + 2]
    pos += 2
    ckv = ckv_ref[...]
    if normalize:
        ckv = _rms_rows(ckv, g_ref[...])
    if emit_norm:
        refs[pos][...] = ckv
    cb = ckv.astype(BF16)
    kn = jnp.dot(cb, wn_ref[...], preferred_element_type=F32)
    v_ref[...] = jnp.dot(cb, wv_ref[...], preferred_element_type=F32).astype(v_ref.dtype)
    kr = kr_ref[...]
    if rope:
        kr = _rope_pad64(kr, cf_ref[...], sa_ref[...], sb_ref[...])
    kr = kr.astype(k_ref.dtype)
    for h in range(C_HEADS):
        lo = h * C_QK_PAD
        k_ref[:, lo:lo + C_NOPE] = kn[:, h * C_NOPE:(h + 1) * C_NOPE].astype(k_ref.dtype)
        k_ref[:, lo + C_NOPE:lo + C_QK_PAD] = kr


def c_keyvalue(ckv_op, kr_op, rows, tm, grid, kv_norm, w_nope, w_v, rope_c=None, rope_spec=None, emit_norm=False):
    ops = [ckv_op, kr_op]
    full = lambda a: (a, pl.BlockSpec(a.shape, lambda *g: (0,) * a.ndim))
    if kv_norm is not None:
        ops.append(full(kv_norm))
    ops += [full(w_nope), full(w_v)]
    if rope_c is not None:
        ops += [(t, rope_spec) for t in rope_c]
    row_map = lambda *g: (functools.reduce(lambda a, b: a * grid[1] + b, g) if len(g) > 1 else g[0], 0)
    out_specs = [pl.BlockSpec((tm, C_HEADS * C_QK_PAD), row_map), pl.BlockSpec((tm, C_HEADS * C_VDIM), row_map)]
    out_shape = [jax.ShapeDtypeStruct((rows, C_HEADS * C_QK_PAD), BF16),
                 jax.ShapeDtypeStruct((rows, C_HEADS * C_VDIM), BF16)]
    if emit_norm:
        out_specs.append(pl.BlockSpec((tm, C_KV_LORA), row_map))
        out_shape.append(jax.ShapeDtypeStruct((rows, C_KV_LORA), F32))
    return pl.pallas_call(
        functools.partial(_ckv_kernel, normalize=kv_norm is not None, rope=rope_c is not None, emit_norm=emit_norm),
        grid=grid, in_specs=[s for _, s in ops], out_specs=out_specs, out_shape=out_shape,
        compiler_params=_cparams(len(grid)), name="c_keyvalue",
    )(*[a for a, _ in ops])


def _merge_kernel(oa_ref, ob_ref, oc_ref, g_ref, x_ref, ga_ref, lng_ref, lnb_ref, wa_ref, wb_ref, wc_ref, wo_ref,
                  o_ref, mix_scr, *, alpha):
    d = D_MODEL
    chunk = 512
    for c in range(d // chunk):
        sl = slice(c * chunk, (c + 1) * chunk)
        ya = jnp.dot(oa_ref[...], wa_ref[:, sl], preferred_element_type=F32)
        yb = jnp.dot(ob_ref[...], wb_ref[:, sl], preferred_element_type=F32)
        yc = jnp.dot(oc_ref[...], wc_ref[:, sl], preferred_element_type=F32)
        mixed = (g_ref[:, c * chunk:(c + 1) * chunk] * ya
                 + g_ref[:, d + c * chunk:d + (c + 1) * chunk] * yb
                 + g_ref[:, 2 * d + c * chunk:2 * d + (c + 1) * chunk] * yc)
        mix_scr[:, sl] = mixed.astype(BF16)
    out = jnp.dot(mix_scr[...], wo_ref[...], preferred_element_type=F32)
    y = alpha * x_ref[...] + ga_ref[...] * out
    o_ref[...] = _layer_norm_rows(y) * lng_ref[...] + lnb_ref[...]


def merge_block(oa, ob, oc, gates, x, x_row0, mods, layer, rows_per_cond, cond_row0, ln_g, ln_b, wa, wb, wc, wo,
                alpha):
    rows, d = oa.shape[0], x.shape[1]
    tm = MERGE_TM
    off = x_row0 // tm
    assert x_row0 % tm == 0
    row_of = lambda i: cond_row0 + (i * tm) // rows_per_cond
    const = lambda a: pl.BlockSpec(a.shape, lambda i: (0, 0), pipeline_mode=pl.Buffered(1))
    rowblk = lambda a: pl.BlockSpec((tm, a.shape[1]), lambda i: (i, 0))
    vec = pl.BlockSpec((1, d), lambda i: (0, 0))
    return pl.pallas_call(
        functools.partial(_merge_kernel, alpha=alpha),
        grid=(rows // tm,),
        in_specs=[rowblk(oa), rowblk(ob), rowblk(oc), rowblk(gates),
                  pl.BlockSpec((tm, d), lambda i: (i + off, 0)), _mod_spec(layer, 2, row_of),
                  vec, vec, const(wa), const(wb), const(wc), const(wo)],
        out_specs=pl.BlockSpec((tm, d), lambda i: (i, 0)),
        out_shape=jax.ShapeDtypeStruct((rows, d), F32),
        scratch_shapes=[pltpu.VMEM((tm, d), BF16)],
        compiler_params=_cparams(1), name="merge",
    )(oa, ob, oc, gates, x, mods, ln_g, ln_b, wa, wb, wc, wo)


ROUTE_RANK_LANE = TOP_K


def _two_group_rows(i, n_ctx_tiles, xc_ref, xl_ref):
    return jnp.where(i < n_ctx_tiles, xc_ref[...], xl_ref[...])


def _two_group_specs(tm, d, n_ctx_tiles, n_lat_tiles):
    return [pl.BlockSpec((tm, d), lambda i, *_: (jnp.minimum(i, n_ctx_tiles - 1), 0)),
            pl.BlockSpec((tm, d), lambda i, *_: (jnp.clip(i - n_ctx_tiles, 0, n_lat_tiles - 1), 0))]


def _two_group_cond_row(tm, rows_ctx, n_seq_lat):
    return lambda i, *_: jnp.where(i * tm < rows_ctx, 0, 1 + (i * tm - rows_ctx) // n_seq_lat)


def _router_kernel(xc_ref, xl_ref, sc_ref, sh_ref, wr_ref, br_ref, h_ref, route_ref, gate_ref, cnt_ref, carry_scr,
                   *, n_ctx_tiles):
    i = pl.program_id(0)

    @pl.when(i == 0)
    def _():
        carry_scr[...] = jnp.zeros_like(carry_scr)

    x = _two_group_rows(i, n_ctx_tiles, xc_ref, xl_ref)
    h = _layer_norm_rows(x) * (1.0 + sc_ref[...]) + sh_ref[...]
    h_ref[...] = h
    logits = jnp.dot(h, wr_ref[...], preferred_element_type=F32, precision=lax.Precision.HIGHEST) + br_ref[...]
    tm = logits.shape[0]
    lane = lax.broadcasted_iota(jnp.int32, (tm, LANE), 1)
    cur = jnp.where(lane < N_EXPERTS, logits, -jnp.inf)
    vals, idxs = [], []
    for _ in range(TOP_K):
        m = jnp.max(cur, axis=-1, keepdims=True)
        ik = jnp.min(jnp.where(cur == m, lane, LANE), axis=-1, keepdims=True)
        vals.append(m)
        idxs.append(ik)
        cur = jnp.where(lane == ik, -jnp.inf, cur)
    e = [jnp.exp(v - vals[0]) for v in vals]
    inv = 1.0 / functools.reduce(jnp.add, e)
    onehot = functools.reduce(jnp.add, [(lane == ik).astype(F32) for ik in idxs])
    r_i = lax.broadcasted_iota(jnp.int32, (tm, tm), 0)
    c_i = lax.broadcasted_iota(jnp.int32, (tm, tm), 1)
    tri = (c_i < r_i).astype(BF16)
    rank_full = jnp.dot(tri, onehot.astype(BF16), preferred_element_type=F32) + carry_scr[...]
    route = jnp.zeros((tm, LANE), jnp.int32)
    gate = jnp.zeros((tm, LANE), F32)
    for k in range(TOP_K):
        rk = jnp.sum(jnp.where(lane == idxs[k], rank_full, 0.0), axis=-1, keepdims=True).astype(jnp.int32)
        route = jnp.where(lane == k, idxs[k], route)
        route = jnp.where(lane == ROUTE_RANK_LANE + k, rk, route)
        gate = jnp.where(lane == k, e[k] * inv, gate)
    route_ref[...] = route
    gate_ref[...] = gate
    carry_scr[...] = carry_scr[...] + jnp.sum(onehot, axis=0, keepdims=True)
    cnt_ref[...] = carry_scr[...]


def router(x_ctx, x_lat, n_seq_lat, mods, layer, w_router_p, b_router_p):
    rows_ctx, d = x_ctx.shape
    rows = rows_ctx + x_lat.shape[0]
    tm = ROUTER_TM
    n_ctx_tiles, n_lat_tiles = rows_ctx // tm, x_lat.shape[0] // tm
    row_of = _two_group_cond_row(tm, rows_ctx, n_seq_lat)
    rowblk = lambda w: pl.BlockSpec((tm, w), lambda i: (i, 0))
    return pl.pallas_call(
        functools.partial(_router_kernel, n_ctx_tiles=n_ctx_tiles), grid=(rows // tm,),
        in_specs=_two_group_specs(tm, d, n_ctx_tiles, n_lat_tiles) + [
            _mod_spec(layer, 4, row_of), _mod_spec(layer, 3, row_of),
            pl.BlockSpec((d, LANE), lambda i: (0, 0)), pl.BlockSpec((1, LANE), lambda i: (0, 0))],
        out_specs=[rowblk(d), rowblk(LANE), rowblk(LANE), pl.BlockSpec((1, LANE), lambda i: (0, 0))],
        out_shape=[jax.ShapeDtypeStruct((rows, d), F32), jax.ShapeDtypeStruct((rows, LANE), jnp.int32),
                   jax.ShapeDtypeStruct((rows, LANE), F32), jax.ShapeDtypeStruct((1, LANE), F32)],
        scratch_shapes=[pltpu.VMEM((1, LANE), F32)],
        compiler_params=_cparams(1), name="router",
    )(x_ctx, x_lat, mods, mods, w_router_p, b_router_p)


def _dispatch_kernel(dest, h_ref, init_hbm, x_hbm, sem):
    del init_hbm
    i = pl.program_id(0)
    tm = h_ref.shape[0]
    for t in range(tm):
        for k in range(TOP_K):
            row = dest[(i * tm + t) * TOP_K + k]
            pltpu.make_async_copy(h_ref.at[pl.ds(t, 1)], x_hbm.at[pl.ds(row, 1)], sem).start(priority=k % 2)
    for k in range(TOP_K):
        pltpu.make_async_copy(h_ref, x_hbm.at[pl.ds(0, tm)], sem).wait()


def dispatch_rows(h, dest, cap):
    t, d = h.shape
    tm = DISPATCH_TM
    grid_spec = pltpu.PrefetchScalarGridSpec(
        num_scalar_prefetch=1, grid=(t // tm,),
        in_specs=[pl.BlockSpec((tm, d), lambda i, dst: (i, 0)), pl.BlockSpec(memory_space=pl.ANY)],
        out_specs=pl.BlockSpec(memory_space=pl.ANY),
        scratch_shapes=[pltpu.SemaphoreType.DMA(())])
    return pl.pallas_call(
        _dispatch_kernel, grid_spec=grid_spec,
        out_shape=jax.ShapeDtypeStruct((cap, d), F32),
        input_output_aliases={2: 0},
        compiler_params=_cparams(1), name="dispatch",
    )(dest, h, jnp.zeros((cap, d), F32))


def _expert_kernel(unit_e, unit_row0, unit_nsb, unit_act,
                   x_hbm, wg_ref, wu_ref, bg_ref, bu_ref, wd_ref, bd_ref, y_hbm,
                   xs, stage, acc, wgb, wub, wdb, sem_in, sem_out, *, n_ff):
    del unit_e
    u = pl.program_id(0)
    j = pl.program_id(1)
    sb = MOE_SB
    nsb = unit_nsb[u]
    row0 = unit_row0[u]
    active = unit_act[u] == 1

    def rows(rb, base=0):
        return pl.ds(pl.multiple_of(base + rb * sb, sb), sb)

    def in_copy(rb, slot):
        return pltpu.make_async_copy(x_hbm.at[rows(rb, row0)], stage.at[slot], sem_in.at[slot])

    def out_copy(rb):
        return pltpu.make_async_copy(acc.at[rows(rb)], y_hbm.at[rows(rb, row0)], sem_out)

    @pl.when((u == 0) & (j == 0))
    def _():
        acc[...] = jnp.zeros_like(acc)

    first = j == 0
    last = j == n_ff - 1

    @pl.when(active & first)
    def _():
        in_copy(0, 0).start()

    @pl.when(active)
    def _():
        wgb[...] = wg_ref[...].astype(BF16)
        wub[...] = wu_ref[...].astype(BF16)
        wdb[...] = wd_ref[...].astype(BF16)

        def ffn(rb, carry):
            @pl.when(first)
            def _():
                slot = rb % 2

                @pl.when(rb + 1 < nsb)
                def _():
                    in_copy(rb + 1, 1 - slot).start()

                in_copy(rb, slot).wait()
                xs[rows(rb), :] = stage[slot].astype(BF16)

            xb = xs[rows(rb), :]
            g = jnp.dot(xb, wgb[...], preferred_element_type=F32) + bg_ref[...]
            v = jnp.dot(xb, wub[...], preferred_element_type=F32) + bu_ref[...]
            g = jnp.minimum(g, SWIGLU_LIMIT)
            v = jnp.clip(v, -SWIGLU_LIMIT, SWIGLU_LIMIT)
            a = (g * _sigmoid(SWIGLU_ALPHA * g) * (v + 1.0)).astype(BF16)
            contrib = jnp.dot(a, wdb[...], preferred_element_type=F32)
            base = jnp.where(first, jnp.broadcast_to(bd_ref[...], contrib.shape), acc[rows(rb), :])
            acc[rows(rb), :] = base + contrib

            @pl.when(last)
            def _():
                out_copy(rb).start()

            return carry

        lax.fori_loop(0, nsb, ffn, 0)

    @pl.when(active & last)
    def _():
        lax.fori_loop(0, nsb, lambda rb, c: (out_copy(rb).wait(), c)[1], 0)


def expert_ffn(x_buf, units, layer, w_gu, b_gu, w_down, b_down):
    cap, d = x_buf.shape
    unit_e, unit_row0, unit_nsb, unit_act = units
    n_units = unit_e.shape[0]
    ff = MOE_FF
    n_ff = D_FF // ff
    max_r = MOE_SB * MOE_MAX_SB

    def wspec(shape, imap):
        def index_map(u, j, ue, r0, nsb, act):
            return imap(ue[u], jnp.where(act[u] == 1, j, n_ff - 1))
        return pl.BlockSpec(shape, index_map)

    grid_spec = pltpu.PrefetchScalarGridSpec(
        num_scalar_prefetch=4,
        grid=(n_units, n_ff),
        in_specs=[
            pl.BlockSpec(memory_space=pl.ANY),
            wspec((None, None, d, ff), lambda e, j: (layer, e, 0, j)),
            wspec((None, None, d, ff), lambda e, j: (layer, e, 0, n_ff + j)),
            wspec((None, None, 1, ff), lambda e, j: (layer, e, 0, j)),
            wspec((None, None, 1, ff), lambda e, j: (layer, e, 0, n_ff + j)),
            wspec((None, None, ff, d), lambda e, j: (layer, e, j, 0)),
            wspec((None, None, 1, d), lambda e, j: (layer, e, 0, 0)),
        ],
        out_specs=pl.BlockSpec(memory_space=pl.ANY),
        scratch_shapes=[pltpu.VMEM((max_r, d), BF16), pltpu.VMEM((2, MOE_SB, d), F32), pltpu.VMEM((max_r, d), F32),
                        pltpu.VMEM((d, ff), BF16), pltpu.VMEM((d, ff), BF16), pltpu.VMEM((ff, d), BF16),
                        pltpu.SemaphoreType.DMA((2,)), pltpu.SemaphoreType.DMA(())],
    )
    return pl.pallas_call(
        functools.partial(_expert_kernel, n_ff=n_ff),
        grid_spec=grid_spec,
        out_shape=jax.ShapeDtypeStruct((cap, d), F32),
        input_output_aliases={4: 0},
        compiler_params=_cparams(2), name="expert_ffn",
    )(unit_e, unit_row0, unit_nsb, unit_act, x_buf, w_gu, w_gu, b_gu, b_gu, w_down, b_down)


def routing_metadata(route, cnt, t):
    sb, max_r = MOE_SB, MOE_SB * MOE_MAX_SB
    idx = route[:, :TOP_K]
    rank = route[:, ROUTE_RANK_LANE:ROUTE_RANK_LANE + TOP_K]
    sizes = cnt[0, :N_EXPERTS].astype(jnp.int32)
    padded = (sizes + sb - 1) // sb * sb
    pend = jnp.cumsum(padded)
    pstart = pend - padded
    dest = (pstart[idx] + rank).reshape(-1).astype(jnp.int32)
    cap = t * TOP_K + N_EXPERTS * sb
    units_e = (padded + max_r - 1) // max_r
    uend = jnp.cumsum(units_e)
    ustart = uend - units_e
    n_units = cap // max_r + N_EXPERTS
    u = jnp.arange(n_units, dtype=jnp.int32)
    n_act = uend[-1]
    active = u < n_act
    ue = jnp.minimum(jnp.searchsorted(uend, u, side='right'), N_EXPERTS - 1).astype(jnp.int32)
    ue = jnp.where(active, ue, ue[jnp.maximum(n_act - 1, 0)])
    q = u - ustart[ue]
    row0 = jnp.where(active, pstart[ue] + q * max_r, 0)
    nsb = jnp.where(active, jnp.clip((padded[ue] - q * max_r) // sb, 0, MOE_MAX_SB), 0)
    return dest, cap, (ue, row0.astype(jnp.int32), nsb.astype(jnp.int32), active.astype(jnp.int32))


def _combine_kernel(dest, y_hbm, gate_ref, xc_ref, xl_ref, gf_ref, lng_ref, lnb_ref, o_ref, ybuf, sem,
                    *, alpha, n_ctx_tiles):
    i = pl.program_id(0)
    n = pl.num_programs(0)
    tm = o_ref.shape[0]

    def gather(blk, slot):
        for t in range(tm):
            for k in range(TOP_K):
                row = dest[(blk * tm + t) * TOP_K + k]
                pltpu.make_async_copy(y_hbm.at[pl.ds(row, 1)], ybuf.at[slot, k, pl.ds(t, 1)], sem.at[slot]).start()

    @pl.when(i == 0)
    def _():
        gather(0, 0)

    @pl.when(i + 1 < n)
    def _():
        gather(i + 1, (i + 1) % 2)

    slot = i % 2
    for k in range(TOP_K):
        pltpu.make_async_copy(y_hbm.at[pl.ds(0, tm)], ybuf.at[slot, k], sem.at[slot]).wait()
    gate = gate_ref[...]
    f = gate[:, 0:1] * ybuf[slot, 0]
    for k in range(1, TOP_K):
        f = f + gate[:, k:k + 1] * ybuf[slot, k]
    x = _two_group_rows(i, n_ctx_tiles, xc_ref, xl_ref)
    y = alpha * x + gf_ref[...] * f
    o_ref[...] = _layer_norm_rows(y) * lng_ref[...] + lnb_ref[...]


def combine_block(y_buf, dest, gate, x_ctx, x_lat, n_seq_lat, mods, layer, ln_g, ln_b, alpha):
    rows_ctx, d = x_ctx.shape
    rows = rows_ctx + x_lat.shape[0]
    tm = COMBINE_TM
    n_ctx_tiles, n_lat_tiles = rows_ctx // tm, x_lat.shape[0] // tm
    row_of = _two_group_cond_row(tm, rows_ctx, n_seq_lat)
    vec = pl.BlockSpec((1, d), lambda i, dst: (0, 0))
    grid_spec = pltpu.PrefetchScalarGridSpec(
        num_scalar_prefetch=1, grid=(rows // tm,),
        in_specs=[pl.BlockSpec(memory_space=pl.ANY),
                  pl.BlockSpec((tm, LANE), lambda i, dst: (i, 0))]
                 + _two_group_specs(tm, d, n_ctx_tiles, n_lat_tiles)
                 + [_mod_spec(layer, 5, row_of), vec, vec],
        out_specs=pl.BlockSpec((tm, d), lambda i, dst: (i, 0)),
        scratch_shapes=[pltpu.VMEM((2, TOP_K, tm, d), F32), pltpu.SemaphoreType.DMA((2,))])
    return pl.pallas_call(
        functools.partial(_combine_kernel, alpha=alpha, n_ctx_tiles=n_ctx_tiles), grid_spec=grid_spec,
        out_shape=jax.ShapeDtypeStruct((rows, d), F32),
        compiler_params=_cparams(1), name="combine",
    )(dest, y_buf, gate, x_ctx, x_lat, mods, ln_g, ln_b)


def _rope_tables(n_tokens):
    pos = jnp.arange(n_tokens)
    row = (pos // GRID_W).astype(F32)
    col = (pos % GRID_W).astype(F32)

    def angles(rot_dim):
        n_freq = rot_dim // 4
        inv = 1.0 / (ROPE_BASE ** (jnp.arange(n_freq, dtype=F32) / n_freq))
        ang = jnp.concatenate([row[:, None] * inv, col[:, None] * inv], -1)
        return jnp.cos(ang), jnp.sin(ang)

    c, s = angles(HEAD_DIM)
    rope_ab = (jnp.concatenate([c, c], -1), jnp.concatenate([-s, s], -1))
    c, s = angles(C_ROPE)
    one = jnp.ones((n_tokens, LANE - C_ROPE), F32)
    zero32 = jnp.zeros_like(s)
    zero64 = jnp.zeros((n_tokens, LANE - C_ROPE), F32)
    rope_c = (jnp.concatenate([c, c, one], -1), jnp.concatenate([-s, zero32, zero64], -1),
              jnp.concatenate([zero32, s, zero64], -1))
    return rope_ab, rope_c


def _mixer_block(x, x_row0, layer, lam_init, alpha, mods, cond_row0, n_batch, n_seq, wts, prm, caches, ropes):
    rows = n_batch * n_seq
    latent = caches is not None
    rope_ab, rope_c = ropes if latent else (None, None)

    rpc = n_seq if latent else rows
    z = ln_mod_matmul(x, x_row0, rows, mods, layer, 0, rpc, cond_row0, wts['w_main'],
                      mode="rope" if latent else "plain", rope=rope_ab, n_seq=n_seq)
    gates = ln_mod_matmul(x, x_row0, rows, mods, layer, 0, rpc, cond_row0, wts['w_gates'], mode="sigmoid",
                          n_seq=n_seq)

    tq = min(512, n_seq)
    nqb = n_seq // tq
    zrow = lambda w, col: pl.BlockSpec((n_seq, w), lambda b, h, i: (b, col(h)))
    zq = lambda w, col: pl.BlockSpec((tq, w), lambda b, h, i: (b * nqb + i, col(h)))

    if latent:
        ck = caches['a_k'].reshape(caches['a_k'].shape[:3] + (A_KV_HEADS * HEAD_DIM,))
        cv = caches['a_v'].reshape(ck.shape)
        oa = banded_attention(z, ck, cv, layer, prm['sink'], n_batch, n_seq)
    else:
        oa = attention(
            (n_batch, A_HEADS, nqb),
            [(z, zq(HEAD_DIM, lambda h: COL_QA // HEAD_DIM + h))],
            [[(z, zrow(HEAD_DIM, lambda h: COL_KA // HEAD_DIM + h // A_GROUP))]],
            [(z, zrow(HEAD_DIM, lambda h: COL_VA // HEAD_DIM + h // A_GROUP))],
            zq(HEAD_DIM, lambda h: h), jax.ShapeDtypeStruct((rows, A_HEADS * HEAD_DIM), BF16),
            scale=HEAD_DIM ** -0.5, sink=prm['sink'], name="attn_a_ctx")

    qb0, kb0 = COL_QB // B_HEAD_DIM, COL_KB // B_HEAD_DIM
    k_ops = [[(z, zrow(B_HEAD_DIM, lambda h, m=m: kb0 + m * B_HEADS + h)) for m in range(2)]]
    v_ops = [(z, zrow(B_V_DIM, lambda h: COL_VB // B_V_DIM + h))]
    if latent:
        l_ctx = caches['b_k'].shape[2]
        cbk = caches['b_k'].reshape(caches['b_k'].shape[:3] + (2 * B_HEADS * B_HEAD_DIM,))
        cbv = caches['b_v'].reshape(caches['b_v'].shape[:3] + (B_HEADS * B_V_DIM,))
        cspec = lambda w, col: pl.BlockSpec((None, None, l_ctx, w), lambda b, h, i: (b, layer, 0, col(h)))
        k_ops = [[(cbk, cspec(B_HEAD_DIM, lambda h, m=m: m * B_HEADS + h)) for m in range(2)]] + k_ops
        v_ops = [(cbv, cspec(B_V_DIM, lambda h: h))] + v_ops
    vec3 = lambda a: (a.reshape(a.shape[0], 1, a.shape[1]),
                      pl.BlockSpec((None, 1, a.shape[1]), lambda b, h, i: (layer, 0, 0)))
    ob = attention(
        (n_batch, B_HEADS, nqb),
        [(z, zq(B_HEAD_DIM, lambda h, m=m: qb0 + m * B_HEADS + h)) for m in range(2)],
        k_ops, v_ops, zq(B_V_DIM, lambda h: h), jax.ShapeDtypeStruct((rows, B_HEADS * B_V_DIM), BF16),
        scale=B_HEAD_DIM ** -0.5,
        diff_ops=[vec3(prm['lam_q1']), vec3(prm['lam_k1']), vec3(prm['lam_q2']), vec3(prm['lam_k2']),
                  vec3(prm['subln'])],
        lam_init=lam_init, name="attn_b")

    q_c = c_query(z, prm['q_norm'], wts['w_uq'], rope_c, n_seq)
    tmc = min(512, n_seq if latent else rows)
    per_seq = max(n_seq // tmc, 1)
    kv_out = c_keyvalue(
        (z, pl.BlockSpec((tmc, C_KV_LORA), lambda i: (i, COL_CKV // C_KV_LORA))),
        (z, pl.BlockSpec((tmc, LANE), lambda i: (i, COL_KR // LANE))),
        rows, tmc, (rows // tmc,), prm['kv_norm'], wts['w_uk'], wts['w_uv'],
        rope_c=rope_c, rope_spec=pl.BlockSpec((tmc, LANE), lambda i: (i % per_seq, 0)), emit_norm=not latent)
    k_c, v_c = kv_out[0], kv_out[1]
    crow = lambda w: pl.BlockSpec((n_seq, w), lambda b, h, i: (b, h))
    cq_spec = lambda w: pl.BlockSpec((tq, w), lambda b, h, i: (b * nqb + i, h))
    k_ops = [[(k_c, crow(C_QK_PAD))]]
    v_ops = [(v_c, crow(C_VDIM))]
    if latent:
        l_ctx = caches['c_kv'].shape[2]
        kx, vx = c_keyvalue(
            (caches['c_kv'], pl.BlockSpec((None, None, l_ctx, C_KV_LORA), lambda b, s: (b, layer, 0, 0))),
            (caches['c_kr'], pl.BlockSpec((None, None, l_ctx, LANE), lambda b, s: (b, layer, 0, 0))),
            n_batch * l_ctx, l_ctx, (n_batch, 1), None, wts['w_uk'], wts['w_uv'])
        xrow = lambda w: pl.BlockSpec((l_ctx, w), lambda b, h, i: (b, h))
        k_ops = [[(kx, xrow(C_QK_PAD))]] + k_ops
        v_ops = [(vx, xrow(C_VDIM))] + v_ops
    oc = attention(
        (n_batch, C_HEADS, nqb), [(q_c, cq_spec(C_QK_PAD))], k_ops, v_ops,
        cq_spec(C_VDIM), jax.ShapeDtypeStruct((rows, C_HEADS * C_VDIM), BF16),
        scale=(C_NOPE + C_ROPE) ** -0.5, name="attn_c")

    x = merge_block(oa, ob, oc, gates, x, x_row0, mods, layer, rpc, cond_row0, prm['ln1_g'], prm['ln1_b'],
                    wts['w_br_a'], wts['w_br_b'], wts['w_br_c'], wts['w_out'], alpha)

    state = None
    if not latent:
        state = (z[:, COL_KA:COL_KA + 256], z[:, COL_VA:COL_VA + 256], z[:, COL_KB:COL_KB + 1024],
                 z[:, COL_VB:COL_VB + 1024], kv_out[2], z[:, COL_KR:COL_KR + C_ROPE])
    return x, state


def _moe_block(x_ctx, x_lat, n_seq_lat, layer, alpha, mods, wts, prm):
    rows = x_ctx.shape[0] + x_lat.shape[0]
    h, route, gate, cnt = router(x_ctx, x_lat, n_seq_lat, mods, layer, wts['w_router'], prm['b_router'])
    dest, cap, units = routing_metadata(route, cnt, rows)
    x_buf = dispatch_rows(h, dest, cap)
    y_buf = expert_ffn(x_buf, units, layer, wts['w_gu'], wts['b_gu'], wts['w_down'], wts['b_down'])
    return combine_block(y_buf, dest, gate, x_ctx, x_lat, n_seq_lat, mods, layer, prm['ln2_g'], prm['ln2_b'], alpha)


def kernel(x_prompt, x_sample, c, cache_a_k, cache_a_v, cache_b_k, cache_b_v, cache_c_kv, cache_c_kr, c_ctx, w_ada, b_ada, w_in, sink_a, lam_q1, lam_k1, lam_q2, lam_k2, subln_b, q_norm_c, kv_norm_c, w_uq, w_ukv, w_br_a, w_br_b, w_br_c, w_out, ln1_g, ln1_b, ln2_g, ln2_b, w_router, b_router, w_gu, b_gu, w_down, b_down):
    depth = w_in.shape[0]
    batch, seq, d = x_prompt.shape
    dec_batch, dec_seq, _ = x_sample.shape
    alpha = (2 * depth) ** 0.25
    assert d == D_MODEL and dec_batch + 1 <= 8

    cond8 = jnp.zeros((8, d), F32).at[0].set(c_ctx).at[1:1 + dec_batch].set(c)
    mods = adaln_all(cond8, w_ada, b_ada).reshape(depth, 8, 6, 1, d)

    ropes = _rope_tables(dec_seq)
    n_exp = w_gu.shape[1]
    b_gu4 = b_gu.reshape(depth, n_exp, 1, 2 * D_FF)
    b_down4 = b_down.reshape(depth, n_exp, 1, d)
    caches = dict(a_k=cache_a_k, a_v=cache_a_v, b_k=cache_b_k, b_v=cache_b_v, c_kv=cache_c_kv,
                  c_kr=jnp.pad(cache_c_kr, ((0, 0), (0, 0), (0, 0), (0, LANE - C_ROPE))))

    rows_ctx = batch * seq
    x_all = jnp.concatenate([x_prompt.reshape(rows_ctx, d), x_sample.reshape(dec_batch * dec_seq, d)], axis=0)
    states = []
    for l in range(depth):
        lam_init = 0.8 - 0.6 * math.exp(-0.3 * l)
        wl = w_in[l]
        w_main = jnp.concatenate([wl[:, :N_IN_SRC], jnp.zeros((d, N_MAIN - N_IN_SRC), F32)], axis=1).astype(BF16)
        w_uq_l = w_uq[l].reshape(C_Q_LORA, C_HEADS, C_NOPE + C_ROPE)
        w_uq_p = jnp.pad(w_uq_l, ((0, 0), (0, 0), (0, C_QK_PAD - C_NOPE - C_ROPE))).reshape(C_Q_LORA, -1).astype(BF16)
        w_ukv_l = w_ukv[l].reshape(C_KV_LORA, C_HEADS, C_NOPE + C_VDIM)
        wts = dict(
            w_main=w_main, w_gates=wl[:, N_IN_SRC:].astype(BF16), w_uq=w_uq_p,
            w_uk=w_ukv_l[:, :, :C_NOPE].reshape(C_KV_LORA, -1).astype(BF16),
            w_uv=w_ukv_l[:, :, C_NOPE:].reshape(C_KV_LORA, -1).astype(BF16),
            w_br_a=w_br_a[l].astype(BF16), w_br_b=w_br_b[l].astype(BF16), w_br_c=w_br_c[l].astype(BF16),
            w_out=w_out[l].astype(BF16),
            w_router=jnp.pad(w_router[l], ((0, 0), (0, LANE - n_exp))),
            w_gu=w_gu, b_gu=b_gu4, w_down=w_down, b_down=b_down4)
        prm = dict(sink=sink_a[l], lam_q1=lam_q1, lam_k1=lam_k1, lam_q2=lam_q2, lam_k2=lam_k2, subln=subln_b,
                   q_norm=q_norm_c[l][None], kv_norm=kv_norm_c[l][None],
                   ln1_g=ln1_g[l][None], ln1_b=ln1_b[l][None], ln2_g=ln2_g[l][None], ln2_b=ln2_b[l][None],
                   b_router=jnp.pad(b_router[l], (0, LANE - n_exp))[None])
        xp, st = _mixer_block(x_all, 0, l, lam_init, alpha, mods, 0, batch, seq, wts, prm, None, None)
        xs, _ = _mixer_block(x_all, rows_ctx, l, lam_init, alpha, mods, 1, dec_batch, dec_seq, wts, prm, caches, ropes)
        states.append(st)
        x_all = _moe_block(xp, xs, dec_seq, l, alpha, mods, wts, prm)

    def stack(i, shape):
        return jnp.stack([s[i].reshape((batch, seq) + shape) for s in states], axis=1)

    return (x_all[:rows_ctx].reshape(batch, seq, d), x_all[rows_ctx:].reshape(dec_batch, dec_seq, d),
            stack(0, (A_KV_HEADS, HEAD_DIM)), stack(1, (A_KV_HEADS, HEAD_DIM)),
            stack(2, (2, B_HEADS, B_HEAD_DIM)), stack(3, (B_HEADS, B_V_DIM)),
            stack(4, (C_KV_LORA,)), stack(5, (C_ROPE,)))
```
